```python
import jax, jax.numpy as jnp
from jax import lax
import numpy as np

D_MODEL = 1024
BATCH = 16
SEQ = 2048
DEPTH = 4

GDN_HEADS = 4
GDN_HEAD_DIM = 128
GDN_WIDTH = GDN_HEADS * GDN_HEAD_DIM
GDN_CONV = 4
GDN_CHUNK = 64
SC_GROUPS = 4
SC_WIDTH = 512
SC_CONV = 3
FOX_HEADS = 4
FOX_HEAD_DIM = 128
FOX_WIDTH = FOX_HEADS * FOX_HEAD_DIM
FOX_BLOCK = 128
N_BRANCH = 3
D_FF = 2816
FFN_CONV = 3
EPS = 1e-6

SPLIT_SIZES = (3 * GDN_WIDTH, GDN_WIDTH, GDN_HEADS, GDN_HEADS,
               3 * SC_WIDTH, 3 * FOX_WIDTH, FOX_HEADS, N_BRANCH * D_MODEL)
PROJ_WIDTH = 3 * GDN_WIDTH + GDN_WIDTH + 2 * GDN_HEADS + 3 * SC_WIDTH + 3 * FOX_WIDTH + FOX_HEADS + N_BRANCH * D_MODEL

kernel_name = "hybrid_gdn_shortconv_fox_convffn"


def rmsnorm(x, g):
    xf = x.astype(jnp.float32)
    y = xf * lax.rsqrt(jnp.mean(xf * xf, axis=-1, keepdims=True) + EPS)
    return (y * g.astype(jnp.float32)).astype(x.dtype)


def l2norm(x):
    xf = x.astype(jnp.float32)
    return xf * lax.rsqrt(jnp.sum(xf * xf, axis=-1, keepdims=True) + EPS)


def causal_dwconv(x, w):
    K, C = w.shape
    return lax.conv_general_dilated(
        x, w[:, None, :].astype(x.dtype), window_strides=(1,), padding=[(K - 1, 0)],
        dimension_numbers=('NWC', 'WIO', 'NWC'), feature_group_count=C)


def split_cols(p):
    pts, acc = [], 0
    for s in SPLIT_SIZES[:-1]:
        acc += s
        pts.append(acc)
    return jnp.split(p, pts, axis=-1)


def gated_delta_rule(q, k, v, g, beta):
    out_dtype = v.dtype
    B, T, H, dk = q.shape
    dv = v.shape[-1]
    C = GDN_CHUNK
    N = T // C
    f32 = jnp.float32

    def chunks(t):
        return t.astype(f32).reshape(B, N, C, H, -1).transpose(0, 3, 1, 2, 4)

    q = chunks(q) * (dk ** -0.5)
    k = chunks(k)
    v = chunks(v)
    beta_c = beta.astype(f32).reshape(B, N, C, H).transpose(0, 3, 1, 2)[..., None]
    gc = jnp.cumsum(g.astype(f32).reshape(B, N, C, H).transpose(0, 3, 1, 2), axis=-1)
    kb = k * beta_c
    vb = v * beta_c

    tril = jnp.tril(jnp.ones((C, C), dtype=bool))
    strict = jnp.tril(jnp.ones((C, C), dtype=bool), -1)
    decay = jnp.exp(jnp.where(tril, gc[..., :, None] - gc[..., None, :], -jnp.inf))

    L = jnp.where(strict, jnp.einsum('bhncd,bhnsd->bhncs', kb, k) * decay, 0.0)
    eye = jnp.eye(C, dtype=f32)
    Tm = lax.linalg.triangular_solve(eye + L, jnp.broadcast_to(eye, L.shape),
                                     left_side=True, lower=True)
    u = jnp.einsum('bhncs,bhnse->bhnce', Tm, vb)
    w = jnp.einsum('bhncs,bhnsd->bhncd', Tm, kb * jnp.exp(gc)[..., None])
    attn = jnp.einsum('bhncd,bhnsd->bhncs', q, k) * decay

    def step(S, xs):
        q_i, k_i, u_i, w_i, g_i, a_i = xs
        v_new = u_i - jnp.einsum('bhcd,bhde->bhce', w_i, S)
        o_i = (jnp.einsum('bhcd,bhde->bhce', q_i * jnp.exp(g_i)[..., None], S)
               + jnp.einsum('bhcs,bhse->bhce', a_i, v_new))
        g_last = g_i[..., -1]
        S = (S * jnp.exp(g_last)[..., None, None]
             + jnp.einsum('bhcd,bhce->bhde', k_i * jnp.exp(g_last[..., None] - g_i)[..., None], v_new))
        return S, o_i

    mv = lambda t: jnp.moveaxis(t, 2, 0)
    S0 = jnp.zeros((B, H, dk, dv), f32)
    _, o = lax.scan(step, S0, (mv(q), mv(k), mv(u), mv(w), mv(gc), mv(attn)))
    return o.transpose(1, 0, 3, 2, 4).reshape(B, T, H, dv).astype(out_dtype)


def forgetting_attention(q, k, v, logf):
    B, T, H, d = q.shape
    scale = d ** -0.5
    c = jnp.cumsum(logf.astype(jnp.float32), axis=1).transpose(0, 2, 1)
    outs = []
    for i in range(T // FOX_BLOCK):
        s0, e = i * FOX_BLOCK, (i + 1) * FOX_BLOCK
        logits = jnp.einsum('bqhd,bkhd->bhqk', q[:, s0:e], k[:, :e]).astype(jnp.float32) * scale
        logits = logits + c[:, :, s0:e, None] - c[:, :, None, :e]
        causal = (s0 + jnp.arange(FOX_BLOCK))[:, None] >= jnp.arange(e)[None, :]
        p = jax.nn.softmax(jnp.where(causal, logits, -jnp.inf), axis=-1)
        outs.append(jnp.einsum('bhqk,bkhd->bqhd', p.astype(v.dtype), v[:, :e]))
    return jnp.concatenate(outs, axis=1)


def hybrid_mixer(xn, w_in, conv_qkv, a_log, dt_bias, gdn_norm, w_br_a,
                 conv_sc, w_br_b, fox_bias, w_br_c, w_o):
    B, T, _ = xn.shape
    proj = xn @ w_in
    qkv_a, z_a, beta_logit, a_logit, bch, qkv_c, f_logit, gate_logits = split_cols(proj)

    qkv_a = jax.nn.silu(causal_dwconv(qkv_a, conv_qkv))
    qa, ka, va = jnp.split(qkv_a, 3, axis=-1)
    hd = (B, T, GDN_HEADS, GDN_HEAD_DIM)
    qa, ka, va = l2norm(qa.reshape(hd)), l2norm(ka.reshape(hd)), va.reshape(hd)
    beta = jax.nn.sigmoid(beta_logit.astype(jnp.float32))
    g = -jnp.exp(a_log.astype(jnp.float32)) * jax.nn.softplus(
        a_logit.astype(jnp.float32) + dt_bias.astype(jnp.float32))
    oa = gated_delta_rule(qa, ka, va, g, beta)
    oa = rmsnorm(oa, gdn_norm) * jax.nn.silu(z_a.reshape(hd))
    ya = oa.reshape(B, T, GDN_WIDTH) @ w_br_a

    b_gate, c_gate, h = jnp.split(bch, 3, axis=-1)
    yb = (b_gate * causal_dwconv(c_gate * h, conv_sc)) @ w_br_b

    qc, kc, vc = jnp.split(qkv_c, 3, axis=-1)
    fd = (B, T, FOX_HEADS, FOX_HEAD_DIM)
    logf = jax.nn.log_sigmoid(f_logit.astype(jnp.float32) + fox_bias.astype(jnp.float32))
    oc = forgetting_attention(qc.reshape(fd), kc.reshape(fd), vc.reshape(fd), logf)
    yc = oc.reshape(B, T, FOX_WIDTH) @ w_br_c

    ga, gb, gcg = jnp.split(jax.nn.sigmoid(gate_logits), 3, axis=-1)
    return (ga * ya + gb * yb + gcg * yc) @ w_o


def conv_glu_ffn(xn, w_up, conv_ffn, w_down):
    h = causal_dwconv(xn @ w_up, conv_ffn)
    gate, up = jnp.split(h, 2, axis=-1)
    return (jax.nn.silu(gate) * up) @ w_down


def setup_inputs(seed: int = 0) -> dict:
    key = jax.random.key(seed)
    ks = jax.random.split(key, 20)
    L, D = DEPTH, D_MODEL
    nrm = lambda k, shape, fan: jax.random.normal(k, shape, jnp.float32) * (fan ** -0.5)
    gain = lambda k, shape: 1.0 + 0.02 * jax.random.normal(k, shape, jnp.float32)
    dt = jnp.exp(jax.random.uniform(ks[4], (L, GDN_HEADS), jnp.float32,
                                    float(np.log(1e-3)), float(np.log(1e-1))))
    return {
        "x": jax.random.normal(ks[0], (BATCH, SEQ, D), jnp.float32),
        "norm1_g": gain(ks[1], (L, D)),
        "w_in": nrm(ks[2], (L, D, PROJ_WIDTH), D),
        "conv_qkv": nrm(ks[3], (L, GDN_CONV, 3 * GDN_WIDTH), GDN_CONV),
        "a_log": jnp.log(jax.random.uniform(ks[5], (L, GDN_HEADS), jnp.float32, 1.0, 16.0)),
        "dt_bias": dt + jnp.log(-jnp.expm1(-dt)),
        "gdn_norm": gain(ks[6], (L, GDN_HEAD_DIM)),
        "w_br_a": nrm(ks[7], (L, GDN_WIDTH, D), GDN_WIDTH),
        "conv_sc": nrm(ks[8], (L, SC_CONV, SC_WIDTH), SC_CONV),
        "w_br_b": nrm(ks[9], (L, SC_WIDTH, D), SC_WIDTH),
        "fox_bias": 2.0 + 0.1 * jax.random.normal(ks[10], (L, FOX_HEADS), jnp.float32),
        "w_br_c": nrm(ks[11], (L, FOX_WIDTH, D), FOX_WIDTH),
        "w_o": nrm(ks[12], (L, D, D), D),
        "norm2_g": gain(ks[13], (L, D)),
        "w_up": nrm(ks[14], (L, D, 2 * D_FF), D),
        "conv_ffn": nrm(ks[15], (L, FFN_CONV, 2 * D_FF), FFN_CONV),
        "w_down": nrm(ks[16], (L, D_FF, D), D_FF),
        "norm_f": gain(ks[17], (D,)),
    }


def reference(x, norm1_g, w_in, conv_qkv, a_log, dt_bias, gdn_norm, w_br_a,
              conv_sc, w_br_b, fox_bias, w_br_c, w_o, norm2_g, w_up, conv_ffn,
              w_down, norm_f):
    h = x
    for l in range(DEPTH):
        h = h + hybrid_mixer(rmsnorm(h, norm1_g[l]), w_in[l], conv_qkv[l], a_log[l], dt_bias[l],
                             gdn_norm[l], w_br_a[l], conv_sc[l], w_br_b[l], fox_bias[l],
                             w_br_c[l], w_o[l])
        h = h + conv_glu_ffn(rmsnorm(h, norm2_g[l]), w_up[l], conv_ffn[l], w_down[l])
    return rmsnorm(h, norm_f)
```

```python
import functools

import jax
import jax.numpy as jnp
from jax import lax
from jax.experimental import pallas as pl
from jax.experimental.pallas import tpu as pltpu

F32 = jnp.float32
BF16 = jnp.bfloat16

D_MODEL = 1024
DEPTH = 4
N_HEADS = 4
HEAD_DIM = 128
WIDTH = N_HEADS * HEAD_DIM
GDN_CONV = 4
GDN_CHUNK = 64
SC_CONV = 3
D_FF = 2816
FFN_CONV = 3
EPS = 1e-6

COL_QKV_A = 0
COL_Z_A = 3 * WIDTH
COL_BCH = 4 * WIDTH
COL_QKV_C = 7 * WIDTH
COL_GATES = 10 * WIDTH
PROJ_COLS = 10 * WIDTH + 3 * D_MODEL
LANE_BETA = 0
LANE_G = 4
LANE_F = 8
SMALL_COLS = 128

ROW_HALO = 8
TOKEN_TILE = 512
GDN_TILE = 256
FOX_TILE = 256
FFN_TILE = 256
VMEM_LIMIT = 56 * 1024 * 1024


def _resident(shape):
    nd = len(shape)
    return pl.BlockSpec(shape, lambda *_: (0,) * nd, pipeline_mode=pl.Buffered(1))


def _params(sem):
    return pltpu.CompilerParams(dimension_semantics=sem, vmem_limit_bytes=VMEM_LIMIT)


def _rms(x, g):
    return x * lax.rsqrt(jnp.mean(x * x, axis=-1, keepdims=True) + EPS) * g


def _sigmoid(x):
    return 1.0 / (1.0 + jnp.exp(-x))


def _dot(a, b):
    return jnp.dot(a, b, preferred_element_type=F32)


def _dot_nt(a, b):
    return lax.dot_general(a, b, (((1,), (1,)), ((), ())), preferred_element_type=F32)


def _dot_tn(a, b):
    return lax.dot_general(a, b, (((0,), (0,)), ((), ())), preferred_element_type=F32)


def _split3(x):
    h1 = x.astype(BF16)
    r1 = x - h1.astype(F32)
    h2 = r1.astype(BF16)
    h3 = (r1 - h2.astype(F32)).astype(BF16)
    return h1, h2, h3


def _dot_f32(a, b):
    a1 = a.astype(BF16)
    a2 = (a - a1.astype(F32)).astype(BF16)
    b1 = b.astype(BF16)
    b2 = (b - b1.astype(F32)).astype(BF16)
    return _dot(a1, b1) + (_dot(a1, b2) + _dot(a2, b1))


def _diff_operands(c):
    n = c.shape[0]
    lane = lax.broadcasted_iota(jnp.int32, (n, HEAD_DIM), 1)
    c1, c2, c3 = (term.astype(F32) for term in _split3(jnp.broadcast_to(c, (n, HEAD_DIM))))
    terms = jnp.where((lane == 0) | (lane == 3), c1, jnp.where((lane == 1) | (lane == 4), c2, c3))
    a = jnp.where(lane < 3, terms, jnp.where(lane < 6, 1.0, 0.0))
    b = jnp.where(lane < 3, 1.0, jnp.where(lane < 6, -terms, 0.0))
    return a.astype(BF16), b.astype(BF16)


def _inproj_kernel(x_ref, g_ref, w_ref, ws_ref, proj_ref, small_ref):
    xn = _rms(x_ref[...], g_ref[...]).astype(BF16)
    for c0 in range(0, PROJ_COLS, D_MODEL):
        proj_ref[:, c0:c0 + D_MODEL] = _dot(xn, w_ref[:, c0:c0 + D_MODEL]).astype(BF16)
    small_ref[...] = _dot(xn, ws_ref[...])


def _inproj(h, g, w_big, w_small, tm):
    n = h.shape[0]
    return pl.pallas_call(
        _inproj_kernel,
        grid=(n // tm,),
        in_specs=[
            pl.BlockSpec((tm, D_MODEL), lambda i: (i, 0)),
            _resident((1, D_MODEL)),
            _resident((D_MODEL, PROJ_COLS)),
            _resident((D_MODEL, SMALL_COLS)),
        ],
        out_specs=[
            pl.BlockSpec((tm, PROJ_COLS), lambda i: (i, 0)),
            pl.BlockSpec((tm, SMALL_COLS), lambda i: (i, 0)),
        ],
        out_shape=[
            jax.ShapeDtypeStruct((n, PROJ_COLS), BF16),
            jax.ShapeDtypeStruct((n, SMALL_COLS), F32),
        ],
        compiler_params=_params(("arbitrary",)),
        name="inproj",
    )(h, g, w_big, w_small)


def _gates_kernel(small_ref, bias_ref, alog_ref, out_ref, carry_ref):
    t = pl.program_id(1)
    tile = small_ref.shape[0]

    @pl.when(t == 0)
    def _():
        carry_ref[...] = jnp.zeros_like(carry_ref)

    x = small_ref[...] + bias_ref[...]
    lane = lax.broadcasted_iota(jnp.int32, x.shape, 1)
    soft = jnp.log1p(jnp.exp(-jnp.abs(x)))
    beta = _sigmoid(x)
    g = -jnp.exp(alog_ref[...]) * (jnp.maximum(x, 0.0) + soft)
    logf = jnp.minimum(x, 0.0) - soft
    vals = jnp.where(lane < LANE_G, beta, jnp.where(lane < LANE_F, g, logf))

    row = lax.broadcasted_iota(jnp.int32, (tile, tile), 0)
    col = lax.broadcasted_iota(jnp.int32, (tile, tile), 1)
    tril = row >= col
    m_full = tril.astype(BF16)
    m_seg = (tril & (row // GDN_CHUNK == col // GDN_CHUNK)).astype(BF16)
    v1, v2, v3 = _split3(vals)
    cum_full = _dot(m_full, v1) + (_dot(m_full, v2) + _dot(m_full, v3)) + carry_ref[0:1, :]
    cum_seg = _dot(m_seg, v1) + (_dot(m_seg, v2) + _dot(m_seg, v3))
    carry_ref[0:1, :] = cum_full[tile - 1:tile, :]
    out_ref[...] = jnp.where(lane < LANE_G, beta, jnp.where(lane < LANE_F, cum_seg, cum_full))


def _gates(small, bias_row, alog_row, batch, seq):
    tile = GDN_TILE
    nt = seq // tile
    return pl.pallas_call(
        _gates_kernel,
        grid=(batch, nt),
        in_specs=[
            pl.BlockSpec((tile, SMALL_COLS), lambda b, t: (b * nt + t, 0)),
            _resident((1, SMALL_COLS)),
            _resident((1, SMALL_COLS)),
        ],
        out_specs=pl.BlockSpec((tile, SMALL_COLS), lambda b, t: (b * nt + t, 0)),
        out_shape=jax.ShapeDtypeStruct(small.shape, F32),
        scratch_shapes=[pltpu.VMEM((ROW_HALO, SMALL_COLS), F32)],
        compiler_params=_params(("arbitrary", "arbitrary")),
        name="gates",
    )(small, bias_row, alog_row)


def _unit_lower_inverse(l_strict):
    c = l_strict.shape[0]
    row = lax.broadcasted_iota(jnp.int32, (c, c), 0)
    col = lax.broadcasted_iota(jnp.int32, (c, c), 1)
    inv = jnp.where(row == col, 1.0, 0.0) - l_strict
    power = l_strict
    span = 1
    while 2 * span < c:
        power = _dot_f32(power, power)
        inv = inv + _dot_f32(inv, power)
        span *= 2
    return inv


def _gdn_kernel(qkv_ref, z_ref, gate_ref, convw_ref, norm_ref, o_ref,
                halo_ref, state_ref, xs_ref, y_ref):
    t = pl.program_id(1)
    tile = qkv_ref.shape[0]

    @pl.when(t == 0)
    def _():
        halo_ref[...] = jnp.zeros_like(halo_ref)
        state_ref[...] = jnp.zeros_like(state_ref)

    x = qkv_ref[...].astype(F32)
    xs_ref[0:ROW_HALO, :] = halo_ref[...]
    xs_ref[ROW_HALO:ROW_HALO + tile, :] = x
    halo_ref[...] = x[tile - ROW_HALO:tile, :]
    y = x * convw_ref[GDN_CONV - 1:GDN_CONV, :]
    for k in range(GDN_CONV - 1):
        shift = GDN_CONV - 1 - k
        y = y + xs_ref[ROW_HALO - shift:ROW_HALO - shift + tile, :] * convw_ref[k:k + 1, :]
    y_ref[...] = y * _sigmoid(y)

    c = GDN_CHUNK
    row = lax.broadcasted_iota(jnp.int32, (c, c), 0)
    col = lax.broadcasted_iota(jnp.int32, (c, c), 1)
    for j in range(tile // c):
        r0 = j * c
        for h in range(N_HEADS):
            lo = h * HEAD_DIM
            q = y_ref[r0:r0 + c, lo:lo + HEAD_DIM]
            k = y_ref[r0:r0 + c, WIDTH + lo:WIDTH + lo + HEAD_DIM]
            v = y_ref[r0:r0 + c, 2 * WIDTH + lo:2 * WIDTH + lo + HEAD_DIM]
            q = q * lax.rsqrt(jnp.sum(q * q, axis=-1, keepdims=True) + EPS) * (HEAD_DIM ** -0.5)
            k = k * lax.rsqrt(jnp.sum(k * k, axis=-1, keepdims=True) + EPS)
            beta = gate_ref[r0:r0 + c, LANE_BETA + h:LANE_BETA + h + 1]
            gc = gate_ref[r0:r0 + c, LANE_G + h:LANE_G + h + 1]
            g_last = gate_ref[r0 + c - 1:r0 + c, LANE_G + h:LANE_G + h + 1]

            da, db = _diff_operands(gc)
            decay = jnp.exp(jnp.where(row >= col, _dot_nt(da, db), -jnp.inf))
            kb = k * beta
            k16 = k.astype(BF16)
            l_strict = jnp.where(row > col, _dot_nt(kb.astype(BF16), k16) * decay, 0.0)
            attn = _dot_nt(q.astype(BF16), k16) * decay
            tm = _unit_lower_inverse(l_strict).astype(BF16)
            u = _dot(tm, (v * beta).astype(BF16))
            w = _dot(tm, (kb * jnp.exp(gc)).astype(BF16))

            s = state_ref[h]
            s16 = s.astype(BF16)
            v_new = u - _dot(w.astype(BF16), s16)
            v16 = v_new.astype(BF16)
            o = _dot((q * jnp.exp(gc)).astype(BF16), s16) + _dot(attn.astype(BF16), v16)
            kd = (k * jnp.exp(g_last - gc)).astype(BF16)
            state_ref[h] = s * jnp.exp(g_last) + _dot_tn(kd, v16)

            z = z_ref[r0:r0 + c, lo:lo + HEAD_DIM].astype(F32)
            o_ref[r0:r0 + c, lo:lo + HEAD_DIM] = (_rms(o, norm_ref[...]) * (z * _sigmoid(z))).astype(BF16)


def _gdn(proj, gate, conv_qkv, gdn_norm, batch, seq):
    tile = GDN_TILE
    nt = seq // tile
    n = proj.shape[0]
    return pl.pallas_call(
        _gdn_kernel,
        grid=(batch, nt),
        in_specs=[
            pl.BlockSpec((tile, 3 * WIDTH), lambda b, t: (b * nt + t, COL_QKV_A // (3 * WIDTH))),
            pl.BlockSpec((tile, WIDTH), lambda b, t: (b * nt + t, COL_Z_A // WIDTH)),
            pl.BlockSpec((tile, SMALL_COLS), lambda b, t: (b * nt + t, 0)),
            _resident((GDN_CONV, 3 * WIDTH)),
            _resident((1, HEAD_DIM)),
        ],
        out_specs=pl.BlockSpec((tile, WIDTH), lambda b, t: (b * nt + t, 0)),
        out_shape=jax.ShapeDtypeStruct((n, WIDTH), BF16),
        scratch_shapes=[
            pltpu.VMEM((ROW_HALO, 3 * WIDTH), F32),
            pltpu.VMEM((N_HEADS, HEAD_DIM, HEAD_DIM), F32),
            pltpu.VMEM((ROW_HALO + tile, 3 * WIDTH), F32),
            pltpu.VMEM((tile, 3 * WIDTH), F32),
        ],
        compiler_params=_params(("arbitrary", "arbitrary")),
        name="gdn",
    )(proj, proj, gate, conv_qkv, gdn_norm)


def _fox_kernel(q_ref, k_ref, v_ref, gate_ref, o_ref, qa_ref, ka_ref):
    seq = q_ref.shape[0]
    tq = FOX_TILE
    scale = HEAD_DIM ** -0.5
    lane = lax.broadcasted_iota(jnp.int32, (seq, SMALL_COLS), 1)
    c = jnp.sum(jnp.where(lane == LANE_F + pl.program_id(1), gate_ref[...], 0.0), axis=-1, keepdims=True)
    ca, cb = _diff_operands(c * (1.0 / scale))
    qa_ref[:, 0:HEAD_DIM] = q_ref[...]
    qa_ref[:, HEAD_DIM:2 * HEAD_DIM] = ca
    ka_ref[:, 0:HEAD_DIM] = k_ref[...]
    ka_ref[:, HEAD_DIM:2 * HEAD_DIM] = cb
    row = lax.broadcasted_iota(jnp.int32, (tq, tq), 0)
    col = lax.broadcasted_iota(jnp.int32, (tq, tq), 1)
    for i in range(seq // tq):
        qa = qa_ref[i * tq:(i + 1) * tq, :]
        m = jnp.full((tq, 1), -jnp.inf, F32)
        den = jnp.zeros((tq, 1), F32)
        acc = jnp.zeros((tq, HEAD_DIM), F32)
        for j in range(i + 1):
            s = _dot_nt(qa, ka_ref[j * tq:(j + 1) * tq, :])
            if j == i:
                s = jnp.where(row >= col, s, -jnp.inf)
            m_new = jnp.maximum(m, jnp.max(s, axis=-1, keepdims=True))
            p = jnp.exp((s - m_new) * scale)
            alpha = jnp.exp((m - m_new) * scale)
            den = alpha * den + jnp.sum(p, axis=-1, keepdims=True)
            acc = alpha * acc + _dot(p.astype(BF16), v_ref[j * tq:(j + 1) * tq, :])
            m = m_new
        o_ref[i * tq:(i + 1) * tq, :] = (acc / den).astype(BF16)


def _fox(proj, gate, batch, seq):
    n = proj.shape[0]
    blk = COL_QKV_C // HEAD_DIM
    return pl.pallas_call(
        _fox_kernel,
        grid=(batch, N_HEADS),
        in_specs=[
            pl.BlockSpec((seq, HEAD_DIM), lambda b, h: (b, blk + h)),
            pl.BlockSpec((seq, HEAD_DIM), lambda b, h: (b, blk + N_HEADS + h)),
            pl.BlockSpec((seq, HEAD_DIM), lambda b, h: (b, blk + 2 * N_HEADS + h)),
            pl.BlockSpec((seq, SMALL_COLS), lambda b, h: (b, 0)),
        ],
        out_specs=pl.BlockSpec((seq, HEAD_DIM), lambda b, h: (b, h)),
        out_shape=jax.ShapeDtypeStruct((n, WIDTH), BF16),
        scratch_shapes=[
            pltpu.VMEM((seq, 2 * HEAD_DIM), BF16),
            pltpu.VMEM((seq, 2 * HEAD_DIM), BF16),
        ],
        compiler_params=_params(("arbitrary", "arbitrary")),
        name="fox",
    )(proj, proj, proj, gate)


def _merge_kernel(oa_ref, oc_ref, bg_ref, cg_ref, hh_ref, ga_ref, gb_ref, gc_ref, h_ref,
                  convw_ref, wa_ref, wb_ref, wc_ref, wo_ref, out_ref, halo_ref, xs_ref,
                  *, tiles_per_seq):
    tile = h_ref.shape[0]

    @pl.when(pl.program_id(0) % tiles_per_seq == 0)
    def _():
        halo_ref[...] = jnp.zeros_like(halo_ref)

    prod = cg_ref[...].astype(F32) * hh_ref[...].astype(F32)
    xs_ref[0:ROW_HALO, :] = halo_ref[...]
    xs_ref[ROW_HALO:ROW_HALO + tile, :] = prod
    halo_ref[...] = prod[tile - ROW_HALO:tile, :]
    conv = prod * convw_ref[SC_CONV - 1:SC_CONV, :]
    for k in range(SC_CONV - 1):
        shift = SC_CONV - 1 - k
        conv = conv + xs_ref[ROW_HALO - shift:ROW_HALO - shift + tile, :] * convw_ref[k:k + 1, :]
    sc = (bg_ref[...].astype(F32) * conv).astype(BF16)

    mix = _sigmoid(ga_ref[...].astype(F32)) * _dot(oa_ref[...], wa_ref[...])
    mix = mix + _sigmoid(gb_ref[...].astype(F32)) * _dot(sc, wb_ref[...])
    mix = mix + _sigmoid(gc_ref[...].astype(F32)) * _dot(oc_ref[...], wc_ref[...])
    out_ref[...] = h_ref[...] + _dot(mix.astype(BF16), wo_ref[...])


def _merge(oa, oc, proj, h, conv_sc, w_a, w_b, w_c, w_o, tm, seq):
    n = h.shape[0]
    bch = COL_BCH // WIDTH
    gates = COL_GATES // D_MODEL
    tok = lambda width, blk: pl.BlockSpec((tm, width), lambda i: (i, blk))
    return pl.pallas_call(
        functools.partial(_merge_kernel, tiles_per_seq=seq // tm),
        grid=(n // tm,),
        in_specs=[
            tok(WIDTH, 0), tok(WIDTH, 0),
            tok(WIDTH, bch), tok(WIDTH, bch + 1), tok(WIDTH, bch + 2),
            tok(D_MODEL, gates), tok(D_MODEL, gates + 1), tok(D_MODEL, gates + 2),
            tok(D_MODEL, 0),
            _resident((SC_CONV, WIDTH)),
            _resident((WIDTH, D_MODEL)), _resident((WIDTH, D_MODEL)), _resident((WIDTH, D_MODEL)),
            _resident((D_MODEL, D_MODEL)),
        ],
        out_specs=tok(D_MODEL, 0),
        out_shape=jax.ShapeDtypeStruct((n, D_MODEL), F32),
        scratch_shapes=[
            pltpu.VMEM((ROW_HALO, WIDTH), F32),
            pltpu.VMEM((ROW_HALO + tm, WIDTH), F32),
        ],
        compiler_params=_params(("arbitrary",)),
        name="merge",
    )(oa, oc, proj, proj, proj, proj, proj, proj, h, conv_sc, w_a, w_b, w_c, w_o)


def _ffn_kernel(h_ref, g_ref, wup_ref, convw_ref, wdown_ref, gf_ref, out_ref, halo_ref, xs_ref,
                *, tiles_per_seq, final_norm):
    tile = h_ref.shape[0]
    tf = FFN_TILE

    @pl.when(pl.program_id(0) % tiles_per_seq == 0)
    def _():
        halo_ref[...] = jnp.zeros_like(halo_ref)

    x = h_ref[...]
    xn = _rms(x, g_ref[...]).astype(BF16)
    acc = jnp.zeros((tile, D_MODEL), F32)
    for j in range(D_FF // tf):
        halves = []
        for part in range(2):
            c0 = part * D_FF + j * tf
            slot = 2 * j + part
            pre = _dot(xn, wup_ref[:, c0:c0 + tf])
            xs_ref[part, 0:ROW_HALO, :] = halo_ref[slot]
            xs_ref[part, ROW_HALO:ROW_HALO + tile, :] = pre
            halo_ref[slot] = pre[tile - ROW_HALO:tile, :]
            conv = pre * convw_ref[FFN_CONV - 1:FFN_CONV, c0:c0 + tf]
            for k in range(FFN_CONV - 1):
                shift = FFN_CONV - 1 - k
                conv = conv + (xs_ref[part, ROW_HALO - shift:ROW_HALO - shift + tile, :]
                               * convw_ref[k:k + 1, c0:c0 + tf])
            halves.append(conv)
        act = (halves[0] * _sigmoid(halves[0]) * halves[1]).astype(BF16)
        acc = acc + _dot(act, wdown_ref[j * tf:(j + 1) * tf, :])
    y = x + acc
    if final_norm:
        y = _rms(y, gf_ref[...])
    out_ref[...] = y


def _ffn(h, g, w_up, conv_ffn, w_down, g_final, tm, seq, final_norm):
    n = h.shape[0]
    return pl.pallas_call(
        functools.partial(_ffn_kernel, tiles_per_seq=seq // tm, final_norm=final_norm),
        grid=(n // tm,),
        in_specs=[
            pl.BlockSpec((tm, D_MODEL), lambda i: (i, 0)),
            _resident((1, D_MODEL)),
            _resident((D_MODEL, 2 * D_FF)),
            _resident((FFN_CONV, 2 * D_FF)),
            _resident((D_FF, D_MODEL)),
            _resident((1, D_MODEL)),
        ],
        out_specs=pl.BlockSpec((tm, D_MODEL), lambda i: (i, 0)),
        out_shape=jax.ShapeDtypeStruct((n, D_MODEL), F32),
        scratch_shapes=[
            pltpu.VMEM((2 * (D_FF // FFN_TILE), ROW_HALO, FFN_TILE), F32),
            pltpu.VMEM((2, ROW_HALO + tm, FFN_TILE), F32),
        ],
        compiler_params=_params(("arbitrary",)),
        name="ffn",
    )(h, g, w_up, conv_ffn, w_down, g_final)


def _reorder_w_in(w):
    beta0 = 4 * WIDTH
    f0 = beta0 + 8 + 6 * WIDTH
    big = jnp.concatenate([w[:, :beta0], w[:, beta0 + 8:f0], w[:, f0 + 4:]], axis=1)
    small = jnp.concatenate(
        [w[:, beta0:beta0 + 8], w[:, f0:f0 + 4], jnp.zeros((w.shape[0], SMALL_COLS - 12), w.dtype)], axis=1)
    return big.astype(BF16), small.astype(BF16)


def _lane_row(pairs):
    row = jnp.zeros((1, SMALL_COLS), F32)
    for lane0, vals in pairs:
        row = row.at[0, lane0:lane0 + N_HEADS].set(vals.astype(F32))
    return row


def kernel(x, norm1_g, w_in, conv_qkv, a_log, dt_bias, gdn_norm, w_br_a, conv_sc, w_br_b, fox_bias, w_br_c, w_o, norm2_g, w_up, conv_ffn, w_down, norm_f):
    batch, seq, d = x.shape
    assert d == D_MODEL and seq % TOKEN_TILE == 0 and seq % GDN_TILE == 0
    tm = TOKEN_TILE
    h = x.reshape(batch * seq, d)
    for l in range(DEPTH):
        w_big, w_small = _reorder_w_in(w_in[l])
        proj, small = _inproj(h, norm1_g[l][None, :], w_big, w_small, tm)
        bias_row = _lane_row([(LANE_G, dt_bias[l]), (LANE_F, fox_bias[l])])
        alog_row = _lane_row([(LANE_G, a_log[l])])
        gate = _gates(small, bias_row, alog_row, batch, seq)
        oa = _gdn(proj, gate, conv_qkv[l], gdn_norm[l][None, :], batch, seq)
        oc = _fox(proj, gate, batch, seq)
        h = _merge(oa, oc, proj, h, conv_sc[l], w_br_a[l].astype(BF16), w_br_b[l].astype(BF16),
                   w_br_c[l].astype(BF16), w_o[l].astype(BF16), tm, seq)
        h = _ffn(h, norm2_g[l][None, :], w_up[l].astype(BF16), conv_ffn[l], w_down[l].astype(BF16),
                 norm_f[None, :], tm, seq, final_norm=(l == DEPTH - 1))
    return h.reshape(batch, seq, d)
```

```python
import functools

import jax
import jax.numpy as jnp
from jax import lax
from jax.experimental import pallas as pl
from jax.experimental.pallas import tpu as pltpu

F32 = jnp.float32
BF16 = jnp.bfloat16

D_MODEL = 1024
DEPTH = 4
N_HEADS = 4
HEAD_DIM = 128
WIDTH = N_HEADS * HEAD_DIM
GDN_CONV = 4
GDN_CHUNK = 64
SC_CONV = 3
D_FF = 2816
FFN_CONV = 3
EPS = 1e-6

COL_QKV_A = 0
COL_Z_A = 3 * WIDTH
COL_BCH = 4 * WIDTH
COL_QKV_C = 7 * WIDTH
COL_GATES = 10 * WIDTH
PROJ_COLS = 10 * WIDTH + 3 * D_MODEL
LANE_BETA = 0
LANE_G = 4
LANE_F = 8
SMALL_COLS = 128

ROW_HALO = 8
TOKEN_TILE = 512
GDN_TILE = 256
FOX_TILE = 256
FFN_TILE = 256
VMEM_LIMIT = 56 * 1024 * 1024


def _resident(shape):
    nd = len(shape)
    return pl.BlockSpec(shape, lambda *_: (0,) * nd, pipeline_mode=pl.Buffered(1))


def _layer_resident(shape, layer):
    nd = len(shape)
    return pl.BlockSpec((None,) + shape, lambda *_: (layer,) + (0,) * nd, pipeline_mode=pl.Buffered(1))


def _params(sem):
    return pltpu.CompilerParams(dimension_semantics=sem, vmem_limit_bytes=VMEM_LIMIT)


def _rms(x, g):
    return x * lax.rsqrt(jnp.mean(x * x, axis=-1, keepdims=True) + EPS) * g


def _sigmoid(x):
    return 1.0 / (1.0 + jnp.exp(-x))


def _dot(a, b):
    return jnp.dot(a, b, preferred_element_type=F32)


def _dot_nt(a, b):
    return lax.dot_general(a, b, (((1,), (1,)), ((), ())), preferred_element_type=F32)


def _dot_tn(a, b):
    return lax.dot_general(a, b, (((0,), (0,)), ((), ())), preferred_element_type=F32)


def _bdot(a, b):
    return lax.dot_general(a, b, (((2,), (1,)), ((0,), (0,))), preferred_element_type=F32)


def _bdot_nt(a, b):
    return lax.dot_general(a, b, (((2,), (2,)), ((0,), (0,))), preferred_element_type=F32)


def _split3(x):
    h1 = x.astype(BF16)
    r1 = x - h1.astype(F32)
    h2 = r1.astype(BF16)
    h3 = (r1 - h2.astype(F32)).astype(BF16)
    return h1, h2, h3


def _bdot_f32(a, b):
    a1 = a.astype(BF16)
    a2 = (a - a1.astype(F32)).astype(BF16)
    b1 = b.astype(BF16)
    b2 = (b - b1.astype(F32)).astype(BF16)
    return _bdot(a1, b1) + (_bdot(a1, b2) + _bdot(a2, b1))


def _diff_operands(c):
    shape = c.shape[:-1] + (HEAD_DIM,)
    lane = lax.broadcasted_iota(jnp.int32, shape, len(shape) - 1)
    c1, c2, c3 = (term.astype(F32) for term in _split3(jnp.broadcast_to(c, shape)))
    terms = jnp.where((lane == 0) | (lane == 3), c1, jnp.where((lane == 1) | (lane == 4), c2, c3))
    a = jnp.where(lane < 3, terms, jnp.where(lane < 6, 1.0, 0.0))
    b = jnp.where(lane < 3, 1.0, jnp.where(lane < 6, -terms, 0.0))
    return a.astype(BF16), b.astype(BF16)


def _inproj_kernel(x_ref, g_ref, w_ref, ws_ref, proj_ref, small_ref):
    xn = _rms(x_ref[...], g_ref[...]).astype(BF16)
    for c0 in range(0, PROJ_COLS, D_MODEL):
        proj_ref[:, c0:c0 + D_MODEL] = _dot(xn, w_ref[:, c0:c0 + D_MODEL]).astype(BF16)
    small_ref[...] = _dot(xn, ws_ref[...])


def _inproj(h, g, w_big, w_small, layer, tm):
    n = h.shape[0]
    return pl.pallas_call(
        _inproj_kernel,
        grid=(n // tm,),
        in_specs=[
            pl.BlockSpec((tm, D_MODEL), lambda i: (i, 0)),
            _resident((1, D_MODEL)),
            _layer_resident((D_MODEL, PROJ_COLS), layer),
            _layer_resident((D_MODEL, SMALL_COLS), layer),
        ],
        out_specs=[
            pl.BlockSpec((tm, PROJ_COLS), lambda i: (i, 0)),
            pl.BlockSpec((tm, SMALL_COLS), lambda i: (i, 0)),
        ],
        out_shape=[
            jax.ShapeDtypeStruct((n, PROJ_COLS), BF16),
            jax.ShapeDtypeStruct((n, SMALL_COLS), F32),
        ],
        compiler_params=_params(("arbitrary",)),
        name="inproj",
    )(h, g, w_big, w_small)


def _gates_kernel(small_ref, bias_ref, alog_ref, out_ref, carry_ref):
    t = pl.program_id(1)
    tile = small_ref.shape[0]

    @pl.when(t == 0)
    def _():
        carry_ref[...] = jnp.zeros_like(carry_ref)

    x = small_ref[...] + bias_ref[...]
    lane = lax.broadcasted_iota(jnp.int32, x.shape, 1)
    soft = jnp.log1p(jnp.exp(-jnp.abs(x)))
    beta = _sigmoid(x)
    g = -jnp.exp(alog_ref[...]) * (jnp.maximum(x, 0.0) + soft)
    logf = jnp.minimum(x, 0.0) - soft
    vals = jnp.where(lane < LANE_G, beta, jnp.where(lane < LANE_F, g, logf))

    row = lax.broadcasted_iota(jnp.int32, (tile, tile), 0)
    col = lax.broadcasted_iota(jnp.int32, (tile, tile), 1)
    tril = row >= col
    m_full = tril.astype(BF16)
    m_seg = (tril & (row // GDN_CHUNK == col // GDN_CHUNK)).astype(BF16)
    v1, v2, v3 = _split3(vals)
    cum_full = _dot(m_full, v1) + (_dot(m_full, v2) + _dot(m_full, v3)) + carry_ref[0:1, :]
    cum_seg = _dot(m_seg, v1) + (_dot(m_seg, v2) + _dot(m_seg, v3))
    carry_ref[0:1, :] = cum_full[tile - 1:tile, :]
    out_ref[...] = jnp.where(lane < LANE_G, beta, jnp.where(lane < LANE_F, cum_seg, cum_full))


def _gates(small, bias_row, alog_row, batch, seq):
    tile = GDN_TILE
    nt = seq // tile
    return pl.pallas_call(
        _gates_kernel,
        grid=(batch, nt),
        in_specs=[
            pl.BlockSpec((tile, SMALL_COLS), lambda b, t: (b * nt + t, 0)),
            _resident((1, SMALL_COLS)),
            _resident((1, SMALL_COLS)),
        ],
        out_specs=pl.BlockSpec((tile, SMALL_COLS), lambda b, t: (b * nt + t, 0)),
        out_shape=jax.ShapeDtypeStruct(small.shape, F32),
        scratch_shapes=[pltpu.VMEM((ROW_HALO, SMALL_COLS), F32)],
        compiler_params=_params(("arbitrary", "arbitrary")),
        name="gates",
    )(small, bias_row, alog_row)


def _unit_lower_inverse(l_strict):
    c = l_strict.shape[-1]
    row = lax.broadcasted_iota(jnp.int32, (c, c), 0)
    col = lax.broadcasted_iota(jnp.int32, (c, c), 1)
    inv = jnp.where(row == col, 1.0, 0.0)[None] - l_strict
    power = l_strict
    span = 1
    while 2 * span < c:
        power = _bdot_f32(power, power)
        inv = inv + _bdot_f32(inv, power)
        span *= 2
    return inv


def _gdn_kernel(qkv_ref, z_ref, gate_ref, convw_ref, norm_ref, o_ref,
                halo_ref, state_ref, xs_ref, y_ref):
    t = pl.program_id(1)
    tile = qkv_ref.shape[0]

    @pl.when(t == 0)
    def _():
        halo_ref[...] = jnp.zeros_like(halo_ref)
        state_ref[...] = jnp.zeros_like(state_ref)

    x = qkv_ref[...].astype(F32)
    xs_ref[0:ROW_HALO, :] = halo_ref[...]
    xs_ref[ROW_HALO:ROW_HALO + tile, :] = x
    halo_ref[...] = x[tile - ROW_HALO:tile, :]
    y = x * convw_ref[GDN_CONV - 1:GDN_CONV, :]
    for k in range(GDN_CONV - 1):
        shift = GDN_CONV - 1 - k
        y = y + xs_ref[ROW_HALO - shift:ROW_HALO - shift + tile, :] * convw_ref[k:k + 1, :]
    y_ref[...] = y * _sigmoid(y)

    c = GDN_CHUNK
    nchunk = tile // c
    problems = [(j, h) for j in range(nchunk) for h in range(N_HEADS)]

    def slabs(col0):
        return jnp.stack([y_ref[j * c:(j + 1) * c, col0 + h * HEAD_DIM:col0 + (h + 1) * HEAD_DIM]
                          for j, h in problems])

    def lanes(lane0, rows=None):
        return jnp.stack([gate_ref[(j * c if rows is None else j * c + rows):(j + 1) * c,
                                   lane0 + h:lane0 + h + 1] for j, h in problems])

    q = slabs(0)
    k = slabs(WIDTH)
    v = slabs(2 * WIDTH)
    q = q * lax.rsqrt(jnp.sum(q * q, axis=-1, keepdims=True) + EPS) * (HEAD_DIM ** -0.5)
    k = k * lax.rsqrt(jnp.sum(k * k, axis=-1, keepdims=True) + EPS)
    beta = lanes(LANE_BETA)
    gc = lanes(LANE_G)
    g_last = lanes(LANE_G, rows=c - 1)

    row = lax.broadcasted_iota(jnp.int32, (c, c), 0)
    col = lax.broadcasted_iota(jnp.int32, (c, c), 1)
    da, db = _diff_operands(gc)
    decay = jnp.exp(jnp.where((row >= col)[None], _bdot_nt(da, db), -jnp.inf))
    kb = k * beta
    k16 = k.astype(BF16)
    l_strict = jnp.where((row > col)[None], _bdot_nt(kb.astype(BF16), k16) * decay, 0.0)
    attn = (_bdot_nt(q.astype(BF16), k16) * decay).astype(BF16)
    tm = _unit_lower_inverse(l_strict).astype(BF16)
    u = _bdot(tm, (v * beta).astype(BF16))
    w = _bdot(tm, (kb * jnp.exp(gc)).astype(BF16)).astype(BF16)
    qg = (q * jnp.exp(gc)).astype(BF16)
    kd = (k * jnp.exp(g_last - gc)).astype(BF16)
    s_decay = jnp.exp(g_last)

    s = state_ref[...]
    for j in range(nchunk):
        b0, b1 = j * N_HEADS, (j + 1) * N_HEADS
        s16 = s.astype(BF16)
        v_new = (u[b0:b1] - _bdot(w[b0:b1], s16)).astype(BF16)
        o = _bdot(qg[b0:b1], s16) + _bdot(attn[b0:b1], v_new)
        s = s * s_decay[b0:b1] + jnp.stack([_dot_tn(kd[b0 + h], v_new[h]) for h in range(N_HEADS)])
        o = _rms(o, norm_ref[...][None])
        for h in range(N_HEADS):
            lo = h * HEAD_DIM
            z = z_ref[j * c:(j + 1) * c, lo:lo + HEAD_DIM].astype(F32)
            o_ref[j * c:(j + 1) * c, lo:lo + HEAD_DIM] = (o[h] * (z * _sigmoid(z))).astype(BF16)
    state_ref[...] = s


def _gdn(proj, gate, conv_qkv, gdn_norm, batch, seq):
    tile = GDN_TILE
    nt = seq // tile
    n = proj.shape[0]
    return pl.pallas_call(
        _gdn_kernel,
        grid=(batch, nt),
        in_specs=[
            pl.BlockSpec((tile, 3 * WIDTH), lambda b, t: (b * nt + t, COL_QKV_A // (3 * WIDTH))),
            pl.BlockSpec((tile, WIDTH), lambda b, t: (b * nt + t, COL_Z_A // WIDTH)),
            pl.BlockSpec((tile, SMALL_COLS), lambda b, t: (b * nt + t, 0)),
            _resident((GDN_CONV, 3 * WIDTH)),
            _resident((1, HEAD_DIM)),
        ],
        out_specs=pl.BlockSpec((tile, WIDTH), lambda b, t: (b * nt + t, 0)),
        out_shape=jax.ShapeDtypeStruct((n, WIDTH), BF16),
        scratch_shapes=[
            pltpu.VMEM((ROW_HALO, 3 * WIDTH), F32),
            pltpu.VMEM((N_HEADS, HEAD_DIM, HEAD_DIM), F32),
            pltpu.VMEM((ROW_HALO + tile, 3 * WIDTH), F32),
            pltpu.VMEM((tile, 3 * WIDTH), F32),
        ],
        compiler_params=_params(("arbitrary", "arbitrary")),
        name="gdn",
    )(proj, proj, gate, conv_qkv, gdn_norm)


def _fox_kernel(q_ref, k_ref, v_ref, gate_ref, o_ref, qa_ref, ka_ref):
    seq = q_ref.shape[0]
    tq = FOX_TILE
    scale = HEAD_DIM ** -0.5
    lane = lax.broadcasted_iota(jnp.int32, (seq, SMALL_COLS), 1)
    c = jnp.sum(jnp.where(lane == LANE_F + pl.program_id(1), gate_ref[...], 0.0), axis=-1, keepdims=True)
    ca, cb = _diff_operands(c * (1.0 / scale))
    qa_ref[:, 0:HEAD_DIM] = q_ref[...]
    qa_ref[:, HEAD_DIM:2 * HEAD_DIM] = ca
    ka_ref[:, 0:HEAD_DIM] = k_ref[...]
    ka_ref[:, HEAD_DIM:2 * HEAD_DIM] = cb
    row = lax.broadcasted_iota(jnp.int32, (tq, tq), 0)
    col = lax.broadcasted_iota(jnp.int32, (tq, tq), 1)
    for i in range(seq // tq):
        qa = qa_ref[i * tq:(i + 1) * tq, :]
        m = jnp.full((tq, 1), -jnp.inf, F32)
        den = jnp.zeros((tq, 1), F32)
        acc = jnp.zeros((tq, HEAD_DIM), F32)
        for j in range(i + 1):
            s = _dot_nt(qa, ka_ref[j * tq:(j + 1) * tq, :])
            if j == i:
                s = jnp.where(row >= col, s, -jnp.inf)
            m_new = jnp.maximum(m, jnp.max(s, axis=-1, keepdims=True))
            p = jnp.exp((s - m_new) * scale)
            alpha = jnp.exp((m - m_new) * scale)
            den = alpha * den + jnp.sum(p, axis=-1, keepdims=True)
            acc = alpha * acc + _dot(p.astype(BF16), v_ref[j * tq:(j + 1) * tq, :])
            m = m_new
        o_ref[i * tq:(i + 1) * tq, :] = (acc / den).astype(BF16)


def _fox(proj, gate, batch, seq):
    n = proj.shape[0]
    blk = COL_QKV_C // HEAD_DIM
    return pl.pallas_call(
        _fox_kernel,
        grid=(batch, N_HEADS),
        in_specs=[
            pl.BlockSpec((seq, HEAD_DIM), lambda b, h: (b, blk + h)),
            pl.BlockSpec((seq, HEAD_DIM), lambda b, h: (b, blk + N_HEADS + h)),
            pl.BlockSpec((seq, HEAD_DIM), lambda b, h: (b, blk + 2 * N_HEADS + h)),
            pl.BlockSpec((seq, SMALL_COLS), lambda b, h: (b, 0)),
        ],
        out_specs=pl.BlockSpec((seq, HEAD_DIM), lambda b, h: (b, h)),
        out_shape=jax.ShapeDtypeStruct((n, WIDTH), BF16),
        scratch_shapes=[
            pltpu.VMEM((seq, 2 * HEAD_DIM), BF16),
            pltpu.VMEM((seq, 2 * HEAD_DIM), BF16),
        ],
        compiler_params=_params(("arbitrary", "arbitrary")),
        name="fox",
    )(proj, proj, proj, gate)


def _merge_kernel(oa_ref, oc_ref, bg_ref, cg_ref, hh_ref, ga_ref, gb_ref, gc_ref, h_ref,
                  convw_ref, wa_ref, wb_ref, wc_ref, wo_ref, out_ref, halo_ref, xs_ref,
                  *, tiles_per_seq):
    tile = h_ref.shape[0]

    @pl.when(pl.program_id(0) % tiles_per_seq == 0)
    def _():
        halo_ref[...] = jnp.zeros_like(halo_ref)

    prod = cg_ref[...].astype(F32) * hh_ref[...].astype(F32)
    xs_ref[0:ROW_HALO, :] = halo_ref[...]
    xs_ref[ROW_HALO:ROW_HALO + tile, :] = prod
    halo_ref[...] = prod[tile - ROW_HALO:tile, :]
    conv = prod * convw_ref[SC_CONV - 1:SC_CONV, :]
    for k in range(SC_CONV - 1):
        shift = SC_CONV - 1 - k
        conv = conv + xs_ref[ROW_HALO - shift:ROW_HALO - shift + tile, :] * convw_ref[k:k + 1, :]
    sc = (bg_ref[...].astype(F32) * conv).astype(BF16)

    mix = _sigmoid(ga_ref[...].astype(F32)) * _dot(oa_ref[...], wa_ref[...])
    mix = mix + _sigmoid(gb_ref[...].astype(F32)) * _dot(sc, wb_ref[...])
    mix = mix + _sigmoid(gc_ref[...].astype(F32)) * _dot(oc_ref[...], wc_ref[...])
    out_ref[...] = h_ref[...] + _dot(mix.astype(BF16), wo_ref[...])


def _merge(oa, oc, proj, h, conv_sc, w_a, w_b, w_c, w_o, layer, tm, seq):
    n = h.shape[0]
    bch = COL_BCH // WIDTH
    gates = COL_GATES // D_MODEL
    tok = lambda width, blk: pl.BlockSpec((tm, width), lambda i: (i, blk))
    return pl.pallas_call(
        functools.partial(_merge_kernel, tiles_per_seq=seq // tm),
        grid=(n // tm,),
        in_specs=[
            tok(WIDTH, 0), tok(WIDTH, 0),
            tok(WIDTH, bch), tok(WIDTH, bch + 1), tok(WIDTH, bch + 2),
            tok(D_MODEL, gates), tok(D_MODEL, gates + 1), tok(D_MODEL, gates + 2),
            tok(D_MODEL, 0),
            _resident((SC_CONV, WIDTH)),
            _layer_resident((WIDTH, D_MODEL), layer), _layer_resident((WIDTH, D_MODEL), layer),
            _layer_resident((WIDTH, D_MODEL), layer), _layer_resident((D_MODEL, D_MODEL), layer),
        ],
        out_specs=tok(D_MODEL, 0),
        out_shape=jax.ShapeDtypeStruct((n, D_MODEL), F32),
        scratch_shapes=[
            pltpu.VMEM((ROW_HALO, WIDTH), F32),
            pltpu.VMEM((ROW_HALO + tm, WIDTH), F32),
        ],
        compiler_params=_params(("arbitrary",)),
        name="merge",
    )(oa, oc, proj, proj, proj, proj, proj, proj, h, conv_sc, w_a, w_b, w_c, w_o)


def _ffn_kernel(h_ref, g_ref, wup_ref, convw_ref, wdown_ref, gf_ref, out_ref, halo_ref, xs_ref,
                *, tiles_per_seq, final_norm):
    tile = h_ref.shape[0]
    tf = FFN_TILE

    @pl.when(pl.program_id(0) % tiles_per_seq == 0)
    def _():
        halo_ref[...] = jnp.zeros_like(halo_ref)

    x = h_ref[...]
    xn = _rms(x, g_ref[...]).astype(BF16)
    acc = jnp.zeros((tile, D_MODEL), F32)
    for j in range(D_FF // tf):
        halves = []
        for part in range(2):
            c0 = part * D_FF + j * tf
            slot = 2 * j + part
            pre = _dot(xn, wup_ref[:, c0:c0 + tf])
            xs_ref[part, 0:ROW_HALO, :] = halo_ref[slot]
            xs_ref[part, ROW_HALO:ROW_HALO + tile, :] = pre
            halo_ref[slot] = pre[tile - ROW_HALO:tile, :]
            conv = pre * convw_ref[FFN_CONV - 1:FFN_CONV, c0:c0 + tf]
            for k in range(FFN_CONV - 1):
                shift = FFN_CONV - 1 - k
                conv = conv + (xs_ref[part, ROW_HALO - shift:ROW_HALO - shift + tile, :]
                               * convw_ref[k:k + 1, c0:c0 + tf])
            halves.append(conv)
        act = (halves[0] * _sigmoid(halves[0]) * halves[1]).astype(BF16)
        acc = acc + _dot(act, wdown_ref[j * tf:(j + 1) * tf, :])
    y = x + acc
    if final_norm:
        y = _rms(y, gf_ref[...])
    out_ref[...] = y


def _ffn(h, g, w_up, conv_ffn, w_down, g_final, layer, tm, seq, final_norm):
    n = h.shape[0]
    return pl.pallas_call(
        functools.partial(_ffn_kernel, tiles_per_seq=seq // tm, final_norm=final_norm),
        grid=(n // tm,),
        in_specs=[
            pl.BlockSpec((tm, D_MODEL), lambda i: (i, 0)),
            _resident((1, D_MODEL)),
            _layer_resident((D_MODEL, 2 * D_FF), layer),
            _resident((FFN_CONV, 2 * D_FF)),
            _layer_resident((D_FF, D_MODEL), layer),
            _resident((1, D_MODEL)),
        ],
        out_specs=pl.BlockSpec((tm, D_MODEL), lambda i: (i, 0)),
        out_shape=jax.ShapeDtypeStruct((n, D_MODEL), F32),
        scratch_shapes=[
            pltpu.VMEM((2 * (D_FF // FFN_TILE), ROW_HALO, FFN_TILE), F32),
            pltpu.VMEM((2, ROW_HALO + tm, FFN_TILE), F32),
        ],
        compiler_params=_params(("arbitrary",)),
        name="ffn",
    )(h, g, w_up, conv_ffn, w_down, g_final)


def _reorder_kernel(main_ref, next_ref, s1_ref, s2_ref, big_ref, small_ref):
    j = pl.program_id(1)
    cat = jnp.concatenate([main_ref[0], next_ref[0]], axis=1)

    def emit(shift):
        big_ref[0] = cat[:, shift:shift + D_MODEL].astype(BF16)

    pl.when(j < COL_BCH // D_MODEL)(lambda: emit(0))
    pl.when((j >= COL_BCH // D_MODEL) & (j < COL_GATES // D_MODEL))(lambda: emit(2 * N_HEADS))
    pl.when(j >= COL_GATES // D_MODEL)(lambda: emit(3 * N_HEADS))
    lane = lax.broadcasted_iota(jnp.int32, s1_ref.shape[1:], 1)
    small_ref[0] = jnp.where(lane < LANE_F, s1_ref[0], jnp.where(lane < LANE_F + N_HEADS, s2_ref[0], 0.0)).astype(BF16)


def _reorder_w_in(w_in):
    depth, d, width = w_in.shape
    scalars_1 = COL_BCH
    scalars_2 = COL_GATES + 2 * N_HEADS
    assert width == PROJ_COLS + 3 * N_HEADS and COL_BCH % D_MODEL == 0 and COL_GATES % D_MODEL == 0
    assert scalars_1 % SMALL_COLS == LANE_BETA and scalars_2 % SMALL_COLS == LANE_F and LANE_F == 2 * N_HEADS
    per_block = D_MODEL // SMALL_COLS
    return pl.pallas_call(
        _reorder_kernel,
        grid=(depth, PROJ_COLS // D_MODEL),
        in_specs=[
            pl.BlockSpec((1, d, D_MODEL), lambda l, j: (l, 0, j)),
            pl.BlockSpec((1, d, SMALL_COLS), lambda l, j: (l, 0, (j + 1) * per_block)),
            pl.BlockSpec((1, d, SMALL_COLS), lambda l, j: (l, 0, scalars_1 // SMALL_COLS)),
            pl.BlockSpec((1, d, SMALL_COLS), lambda l, j: (l, 0, scalars_2 // SMALL_COLS)),
        ],
        out_specs=[
            pl.BlockSpec((1, d, D_MODEL), lambda l, j: (l, 0, j)),
            pl.BlockSpec((1, d, SMALL_COLS), lambda l, j: (l, 0, 0)),
        ],
        out_shape=[
            jax.ShapeDtypeStruct((depth, d, PROJ_COLS), BF16),
            jax.ShapeDtypeStruct((depth, d, SMALL_COLS), BF16),
        ],
        compiler_params=_params(("arbitrary", "arbitrary")),
        name="reorder_w_in",
    )(w_in, w_in, w_in, w_in)


def _lane_row(pairs):
    row = jnp.zeros((1, SMALL_COLS), F32)
    for lane0, vals in pairs:
        row = row.at[0, lane0:lane0 + N_HEADS].set(vals.astype(F32))
    return row


def kernel(x, norm1_g, w_in, conv_qkv, a_log, dt_bias, gdn_norm, w_br_a, conv_sc, w_br_b, fox_bias, w_br_c, w_o, norm2_g, w_up, conv_ffn, w_down, norm_f):
    batch, seq, d = x.shape
    assert d == D_MODEL and seq % TOKEN_TILE == 0 and seq % GDN_TILE == 0
    tm = TOKEN_TILE
    h = x.reshape(batch * seq, d)
    w_big, w_small = _reorder_w_in(w_in)
    w_a, w_b, w_c, w_out = (w.astype(BF16) for w in (w_br_a, w_br_b, w_br_c, w_o))
    w_up16, w_down16 = w_up.astype(BF16), w_down.astype(BF16)
    for l in range(DEPTH):
        proj, small = _inproj(h, norm1_g[l][None, :], w_big, w_small, l, tm)
        bias_row = _lane_row([(LANE_G, dt_bias[l]), (LANE_F, fox_bias[l])])
        alog_row = _lane_row([(LANE_G, a_log[l])])
        gate = _gates(small, bias_row, alog_row, batch, seq)
        oa = _gdn(proj, gate, conv_qkv[l], gdn_norm[l][None, :], batch, seq)
        oc = _fox(proj, gate, batch, seq)
        h = _merge(oa, oc, proj, h, conv_sc[l], w_a, w_b, w_c, w_out, l, tm, seq)
        h = _ffn(h, norm2_g[l][None, :], w_up16, conv_ffn[l], w_down16, norm_f[None, :], l, tm, seq,
                 final_norm=(l == DEPTH - 1))
    return h.reshape(batch, seq, d)
```

```python
import functools

import jax
import jax.numpy as jnp
from jax import lax
from jax.experimental import pallas as pl
from jax.experimental.pallas import tpu as pltpu

F32 = jnp.float32
BF16 = jnp.bfloat16

D_MODEL = 1024
DEPTH = 4
N_HEADS = 4
HEAD_DIM = 128
WIDTH = N_HEADS * HEAD_DIM
GDN_CONV = 4
GDN_CHUNK = 64
SC_CONV = 3
D_FF = 2816
FFN_CONV = 3
EPS = 1e-6

COL_QKV_A = 0
COL_Z_A = 3 * WIDTH
COL_BCH = 4 * WIDTH
COL_QKV_C = 7 * WIDTH
COL_GATES = 10 * WIDTH
PROJ_COLS = 10 * WIDTH + 3 * D_MODEL
LANE_BETA = 0
LANE_G = 4
LANE_F = 8
SMALL_COLS = 128

ROW_HALO = 8
TOKEN_TILE = 512
GDN_TILE = 256
FOX_TILE = 256
FFN_TILE = 256
VMEM_LIMIT = 56 * 1024 * 1024


def _resident(shape):
    nd = len(shape)
    return pl.BlockSpec(shape, lambda *_: (0,) * nd, pipeline_mode=pl.Buffered(1))


def _layer_resident(shape, layer):
    nd = len(shape)
    return pl.BlockSpec((None,) + shape, lambda *_: (layer,) + (0,) * nd, pipeline_mode=pl.Buffered(1))


def _params(sem):
    return pltpu.CompilerParams(dimension_semantics=sem, vmem_limit_bytes=VMEM_LIMIT)


def _rms(x, g):
    return x * lax.rsqrt(jnp.mean(x * x, axis=-1, keepdims=True) + EPS) * g


def _sigmoid(x):
    return 1.0 / (1.0 + jnp.exp(-x))


def _dot(a, b):
    return jnp.dot(a, b, preferred_element_type=F32)


def _dot_nt(a, b):
    return lax.dot_general(a, b, (((1,), (1,)), ((), ())), preferred_element_type=F32)


def _dot_tn(a, b):
    return lax.dot_general(a, b, (((0,), (0,)), ((), ())), preferred_element_type=F32)


def _bdot(a, b):
    return lax.dot_general(a, b, (((2,), (1,)), ((0,), (0,))), preferred_element_type=F32)


def _bdot_nt(a, b):
    return lax.dot_general(a, b, (((2,), (2,)), ((0,), (0,))), preferred_element_type=F32)


def _split3(x):
    h1 = x.astype(BF16)
    r1 = x - h1.astype(F32)
    h2 = r1.astype(BF16)
    h3 = (r1 - h2.astype(F32)).astype(BF16)
    return h1, h2, h3


def _bdot_f32(a, b):
    a1 = a.astype(BF16)
    a2 = (a - a1.astype(F32)).astype(BF16)
    b1 = b.astype(BF16)
    b2 = (b - b1.astype(F32)).astype(BF16)
    return _bdot(a1, b1) + (_bdot(a1, b2) + _bdot(a2, b1))


def _diff_operands(c):
    shape = c.shape[:-1] + (HEAD_DIM,)
    lane = lax.broadcasted_iota(jnp.int32, shape, len(shape) - 1)
    c1, c2, c3 = (term.astype(F32) for term in _split3(jnp.broadcast_to(c, shape)))
    terms = jnp.where((lane == 0) | (lane == 3), c1, jnp.where((lane == 1) | (lane == 4), c2, c3))
    a = jnp.where(lane < 3, terms, jnp.where(lane < 6, 1.0, 0.0))
    b = jnp.where(lane < 3, 1.0, jnp.where(lane < 6, -terms, 0.0))
    return a.astype(BF16), b.astype(BF16)


def _causal_conv(pre, halo_ref, xs_ref, taps):
    tile = pre.shape[0]
    ntap = taps.shape[0]
    xs_ref[0:ROW_HALO, :] = halo_ref[...]
    xs_ref[ROW_HALO:ROW_HALO + tile, :] = pre
    halo_ref[...] = pre[tile - ROW_HALO:tile, :]
    out = pre * taps[ntap - 1:ntap, :]
    for k in range(ntap - 1):
        shift = ntap - 1 - k
        out = out + xs_ref[ROW_HALO - shift:ROW_HALO - shift + tile, :] * taps[k:k + 1, :]
    return out


def _inproj_kernel(x_ref, g_ref, w_ref, ws_ref, convw_ref, proj_ref, small_ref, halo_ref, xs_ref,
                   *, tiles_per_seq):
    @pl.when(pl.program_id(0) % tiles_per_seq == 0)
    def _():
        halo_ref[...] = jnp.zeros_like(halo_ref)

    xn = _rms(x_ref[...], g_ref[...]).astype(BF16)
    for part in range(3):
        c0 = COL_QKV_A + part * WIDTH
        pre = _dot(xn, w_ref[:, c0:c0 + WIDTH])
        y = _causal_conv(pre, halo_ref.at[part], xs_ref.at[part], convw_ref[:, c0:c0 + WIDTH])
        y = y * _sigmoid(y)
        for h in range(N_HEADS):
            lo = h * HEAD_DIM
            yh = y[:, lo:lo + HEAD_DIM]
            if part < 2:
                yh = yh * lax.rsqrt(jnp.sum(yh * yh, axis=-1, keepdims=True) + EPS)
            if part == 0:
                yh = yh * (HEAD_DIM ** -0.5)
            proj_ref[:, c0 + lo:c0 + lo + HEAD_DIM] = yh.astype(BF16)
    proj_ref[:, COL_Z_A:COL_BCH] = _dot(xn, w_ref[:, COL_Z_A:COL_BCH]).astype(BF16)
    for c0 in range(COL_BCH, PROJ_COLS, D_MODEL):
        proj_ref[:, c0:c0 + D_MODEL] = _dot(xn, w_ref[:, c0:c0 + D_MODEL]).astype(BF16)
    small_ref[...] = _dot(xn, ws_ref[...])


def _inproj(h, g, w_big, w_small, conv_qkv, layer, tm, seq):
    n = h.shape[0]
    return pl.pallas_call(
        functools.partial(_inproj_kernel, tiles_per_seq=seq // tm),
        grid=(n // tm,),
        in_specs=[
            pl.BlockSpec((tm, D_MODEL), lambda i: (i, 0)),
            _resident((1, D_MODEL)),
            _layer_resident((D_MODEL, PROJ_COLS), layer),
            _layer_resident((D_MODEL, SMALL_COLS), layer),
            _resident((GDN_CONV, 3 * WIDTH)),
        ],
        out_specs=[
            pl.BlockSpec((tm, PROJ_COLS), lambda i: (i, 0)),
            pl.BlockSpec((tm, SMALL_COLS), lambda i: (i, 0)),
        ],
        out_shape=[
            jax.ShapeDtypeStruct((n, PROJ_COLS), BF16),
            jax.ShapeDtypeStruct((n, SMALL_COLS), F32),
        ],
        scratch_shapes=[
            pltpu.VMEM((3, ROW_HALO, WIDTH), F32),
            pltpu.VMEM((3, ROW_HALO + tm, WIDTH), F32),
        ],
        compiler_params=_params(("arbitrary",)),
        name="inproj",
    )(h, g, w_big, w_small, conv_qkv)


def _gates_kernel(small_ref, bias_ref, alog_ref, out_ref, carry_ref):
    t = pl.program_id(1)
    tile = small_ref.shape[0]

    @pl.when(t == 0)
    def _():
        carry_ref[...] = jnp.zeros_like(carry_ref)

    x = small_ref[...] + bias_ref[...]
    lane = lax.broadcasted_iota(jnp.int32, x.shape, 1)
    soft = jnp.log1p(jnp.exp(-jnp.abs(x)))
    beta = _sigmoid(x)
    g = -jnp.exp(alog_ref[...]) * (jnp.maximum(x, 0.0) + soft)
    logf = jnp.minimum(x, 0.0) - soft
    vals = jnp.where(lane < LANE_G, beta, jnp.where(lane < LANE_F, g, logf))

    row = lax.broadcasted_iota(jnp.int32, (tile, tile), 0)
    col = lax.broadcasted_iota(jnp.int32, (tile, tile), 1)
    tril = row >= col
    m_full = tril.astype(BF16)
    m_seg = (tril & (row // GDN_CHUNK == col // GDN_CHUNK)).astype(BF16)
    v1, v2, v3 = _split3(vals)
    cum_full = _dot(m_full, v1) + (_dot(m_full, v2) + _dot(m_full, v3)) + carry_ref[0:1, :]
    cum_seg = _dot(m_seg, v1) + (_dot(m_seg, v2) + _dot(m_seg, v3))
    carry_ref[0:1, :] = cum_full[tile - 1:tile, :]
    out_ref[...] = jnp.where(lane < LANE_G, beta, jnp.where(lane < LANE_F, cum_seg, cum_full))


def _gates(small, bias_row, alog_row, batch, seq):
    tile = GDN_TILE
    nt = seq // tile
    return pl.pallas_call(
        _gates_kernel,
        grid=(batch, nt),
        in_specs=[
            pl.BlockSpec((tile, SMALL_COLS), lambda b, t: (b * nt + t, 0)),
            _resident((1, SMALL_COLS)),
            _resident((1, SMALL_COLS)),
        ],
        out_specs=pl.BlockSpec((tile, SMALL_COLS), lambda b, t: (b * nt + t, 0)),
        out_shape=jax.ShapeDtypeStruct(small.shape, F32),
        scratch_shapes=[pltpu.VMEM((ROW_HALO, SMALL_COLS), F32)],
        compiler_params=_params(("arbitrary", "arbitrary")),
        name="gates",
    )(small, bias_row, alog_row)


def _unit_lower_inverse(l_strict):
    c = l_strict.shape[-1]
    row = lax.broadcasted_iota(jnp.int32, (c, c), 0)
    col = lax.broadcasted_iota(jnp.int32, (c, c), 1)
    power = -l_strict
    prod = jnp.where(row == col, 1.0, 0.0)[None] + power
    span = 1
    while 2 * span < c:
        power = _bdot_f32(power, power) if span == 1 else stacked[:, :c]
        span *= 2
        if 2 * span < c:
            stacked = _bdot_f32(jnp.concatenate([power, prod], axis=1), power)
            prod = prod + stacked[:, c:]
        else:
            prod = prod + _bdot_f32(prod, power)
    return prod


def _gdn_kernel(qkv_ref, z_ref, gate_ref, norm_ref, o_ref, state_ref):
    tile = qkv_ref.shape[0]

    @pl.when(pl.program_id(1) == 0)
    def _():
        state_ref[...] = jnp.zeros_like(state_ref)

    c = GDN_CHUNK
    nchunk = tile // c
    problems = [(j, h) for j in range(nchunk) for h in range(N_HEADS)]

    def slabs(col0):
        return jnp.stack([qkv_ref[j * c:(j + 1) * c, col0 + h * HEAD_DIM:col0 + (h + 1) * HEAD_DIM]
                          for j, h in problems])

    def lanes(lane0, rows=None):
        return jnp.stack([gate_ref[(j * c if rows is None else j * c + rows):(j + 1) * c,
                                   lane0 + h:lane0 + h + 1] for j, h in problems])

    q = slabs(0).astype(F32)
    k16 = slabs(WIDTH)
    k = k16.astype(F32)
    v = slabs(2 * WIDTH).astype(F32)
    beta = lanes(LANE_BETA)
    gc = lanes(LANE_G)
    g_last = lanes(LANE_G, rows=c - 1)

    row = lax.broadcasted_iota(jnp.int32, (c, c), 0)
    col = lax.broadcasted_iota(jnp.int32, (c, c), 1)
    da, db = _diff_operands(gc)
    decay = jnp.exp(jnp.where((row >= col)[None], _bdot_nt(da, db), -jnp.inf))
    kb = k * beta
    l_strict = jnp.where((row > col)[None], _bdot_nt(kb.astype(BF16), k16) * decay, 0.0)
    attn = (_bdot_nt(slabs(0), k16) * decay).astype(BF16)
    tm = _unit_lower_inverse(l_strict).astype(BF16)
    u = _bdot(tm, (v * beta).astype(BF16))
    w = _bdot(tm, (kb * jnp.exp(gc)).astype(BF16)).astype(BF16)
    qg = (q * jnp.exp(gc)).astype(BF16)
    kd = (k * jnp.exp(g_last - gc)).astype(BF16)
    s_decay = jnp.exp(g_last)

    u_hi = u.astype(BF16)
    u_lo = (u - u_hi.astype(F32)).astype(BF16)
    wuu = jnp.concatenate([w, u_hi, u_lo], axis=-1)
    kd_t = jnp.stack([_dot_tn(kd[p], wuu[p]) for p in range(len(problems))])
    kd_w = kd_t[..., 0:HEAD_DIM]
    kd_u = kd_t[..., HEAD_DIM:2 * HEAD_DIM] + kd_t[..., 2 * HEAD_DIM:3 * HEAD_DIM]
    kd_w_hi = kd_w.astype(BF16)
    kd_w_split = jnp.concatenate([kd_w_hi, (kd_w - kd_w_hi.astype(F32)).astype(BF16)], axis=-1)

    s = state_ref[...]
    starts = []
    for j in range(nchunk):
        b0, b1 = j * N_HEADS, (j + 1) * N_HEADS
        s16 = s.astype(BF16)
        starts.append(s16)
        s = s * s_decay[b0:b1] + kd_u[b0:b1] - _bdot(kd_w_split[b0:b1], jnp.concatenate([s16, s16], axis=1))
    state_ref[...] = s

    s_start = jnp.concatenate(starts, axis=0)
    v_new = (u - _bdot(w, s_start)).astype(BF16)
    o = _rms(_bdot(qg, s_start) + _bdot(attn, v_new), norm_ref[...][None])
    for p, (j, h) in enumerate(problems):
        lo = h * HEAD_DIM
        z = z_ref[j * c:(j + 1) * c, lo:lo + HEAD_DIM].astype(F32)
        o_ref[j * c:(j + 1) * c, lo:lo + HEAD_DIM] = (o[p] * (z * _sigmoid(z))).astype(BF16)


def _gdn(proj, gate, gdn_norm, batch, seq):
    tile = GDN_TILE
    nt = seq // tile
    n = proj.shape[0]
    return pl.pallas_call(
        _gdn_kernel,
        grid=(batch, nt),
        in_specs=[
            pl.BlockSpec((tile, 3 * WIDTH), lambda b, t: (b * nt + t, COL_QKV_A // (3 * WIDTH))),
            pl.BlockSpec((tile, WIDTH), lambda b, t: (b * nt + t, COL_Z_A // WIDTH)),
            pl.BlockSpec((tile, SMALL_COLS), lambda b, t: (b * nt + t, 0)),
            _resident((1, HEAD_DIM)),
        ],
        out_specs=pl.BlockSpec((tile, WIDTH), lambda b, t: (b * nt + t, 0)),
        out_shape=jax.ShapeDtypeStruct((n, WIDTH), BF16),
        scratch_shapes=[pltpu.VMEM((N_HEADS, HEAD_DIM, HEAD_DIM), F32)],
        compiler_params=_params(("arbitrary", "arbitrary")),
        name="gdn",
    )(proj, proj, gate, gdn_norm)


def _fox_kernel(q_ref, k_ref, v_ref, gate_ref, o_ref, qa_ref, ka_ref):
    seq = q_ref.shape[0]
    tq = FOX_TILE
    scale = HEAD_DIM ** -0.5
    lane = lax.broadcasted_iota(jnp.int32, (seq, SMALL_COLS), 1)
    c = jnp.sum(jnp.where(lane == LANE_F + pl.program_id(1), gate_ref[...], 0.0), axis=-1, keepdims=True)
    ca, cb = _diff_operands(c * (1.0 / scale))
    qa_ref[:, 0:HEAD_DIM] = q_ref[...]
    qa_ref[:, HEAD_DIM:2 * HEAD_DIM] = ca
    ka_ref[:, 0:HEAD_DIM] = k_ref[...]
    ka_ref[:, HEAD_DIM:2 * HEAD_DIM] = cb
    row = lax.broadcasted_iota(jnp.int32, (tq, tq), 0)
    col = lax.broadcasted_iota(jnp.int32, (tq, tq), 1)
    for i in range(seq // tq):
        qa = qa_ref[i * tq:(i + 1) * tq, :]
        m = jnp.full((tq, 1), -jnp.inf, F32)
        den = jnp.zeros((tq, 1), F32)
        acc = jnp.zeros((tq, HEAD_DIM), F32)
        for j in range(i + 1):
            s = _dot_nt(qa, ka_ref[j * tq:(j + 1) * tq, :])
            if j == i:
                s = jnp.where(row >= col, s, -jnp.inf)
            m_new = jnp.maximum(m, jnp.max(s, axis=-1, keepdims=True))
            p = jnp.exp((s - m_new) * scale)
            alpha = jnp.exp((m - m_new) * scale)
            den = alpha * den + jnp.sum(p, axis=-1, keepdims=True)
            acc = alpha * acc + _dot(p.astype(BF16), v_ref[j * tq:(j + 1) * tq, :])
            m = m_new
        o_ref[i * tq:(i + 1) * tq, :] = (acc / den).astype(BF16)


def _fox(proj, gate, batch, seq):
    n = proj.shape[0]
    blk = COL_QKV_C // HEAD_DIM
    return pl.pallas_call(
        _fox_kernel,
        grid=(batch, N_HEADS),
        in_specs=[
            pl.BlockSpec((seq, HEAD_DIM), lambda b, h: (b, blk + h)),
            pl.BlockSpec((seq, HEAD_DIM), lambda b, h: (b, blk + N_HEADS + h)),
            pl.BlockSpec((seq, HEAD_DIM), lambda b, h: (b, blk + 2 * N_HEADS + h)),
            pl.BlockSpec((seq, SMALL_COLS), lambda b, h: (b, 0)),
        ],
        out_specs=pl.BlockSpec((seq, HEAD_DIM), lambda b, h: (b, h)),
        out_shape=jax.ShapeDtypeStruct((n, WIDTH), BF16),
        scratch_shapes=[
            pltpu.VMEM((seq, 2 * HEAD_DIM), BF16),
            pltpu.VMEM((seq, 2 * HEAD_DIM), BF16),
        ],
        compiler_params=_params(("arbitrary", "arbitrary")),
        name="fox",
    )(proj, proj, proj, gate)


def _merge_kernel(oa_ref, oc_ref, bg_ref, cg_ref, hh_ref, ga_ref, gb_ref, gc_ref, h_ref,
                  convw_ref, wa_ref, wb_ref, wc_ref, wo_ref, out_ref, halo_ref, xs_ref,
                  *, tiles_per_seq):
    tile = h_ref.shape[0]

    @pl.when(pl.program_id(0) % tiles_per_seq == 0)
    def _():
        halo_ref[...] = jnp.zeros_like(halo_ref)

    conv = _causal_conv(cg_ref[...].astype(F32) * hh_ref[...].astype(F32), halo_ref, xs_ref, convw_ref[...])
    sc = (bg_ref[...].astype(F32) * conv).astype(BF16)

    mix = _sigmoid(ga_ref[...].astype(F32)) * _dot(oa_ref[...], wa_ref[...])
    mix = mix + _sigmoid(gb_ref[...].astype(F32)) * _dot(sc, wb_ref[...])
    mix = mix + _sigmoid(gc_ref[...].astype(F32)) * _dot(oc_ref[...], wc_ref[...])
    out_ref[...] = h_ref[...] + _dot(mix.astype(BF16), wo_ref[...])


def _merge(oa, oc, proj, h, conv_sc, w_a, w_b, w_c, w_o, layer, tm, seq):
    n = h.shape[0]
    bch = COL_BCH // WIDTH
    gates = COL_GATES // D_MODEL
    tok = lambda width, blk: pl.BlockSpec((tm, width), lambda i: (i, blk))
    return pl.pallas_call(
        functools.partial(_merge_kernel, tiles_per_seq=seq // tm),
        grid=(n // tm,),
        in_specs=[
            tok(WIDTH, 0), tok(WIDTH, 0),
            tok(WIDTH, bch), tok(WIDTH, bch + 1), tok(WIDTH, bch + 2),
            tok(D_MODEL, gates), tok(D_MODEL, gates + 1), tok(D_MODEL, gates + 2),
            tok(D_MODEL, 0),
            _resident((SC_CONV, WIDTH)),
            _layer_resident((WIDTH, D_MODEL), layer), _layer_resident((WIDTH, D_MODEL), layer),
            _layer_resident((WIDTH, D_MODEL), layer), _layer_resident((D_MODEL, D_MODEL), layer),
        ],
        out_specs=tok(D_MODEL, 0),
        out_shape=jax.ShapeDtypeStruct((n, D_MODEL), F32),
        scratch_shapes=[
            pltpu.VMEM((ROW_HALO, WIDTH), F32),
            pltpu.VMEM((ROW_HALO + tm, WIDTH), F32),
        ],
        compiler_params=_params(("arbitrary",)),
        name="merge",
    )(oa, oc, proj, proj, proj, proj, proj, proj, h, conv_sc, w_a, w_b, w_c, w_o)


def _ffn_kernel(h_ref, g_ref, wup_ref, convw_ref, wdown_ref, gf_ref, out_ref, halo_ref, xs_ref,
                *, tiles_per_seq, final_norm):
    tile = h_ref.shape[0]
    tf = FFN_TILE

    @pl.when(pl.program_id(0) % tiles_per_seq == 0)
    def _():
        halo_ref[...] = jnp.zeros_like(halo_ref)

    x = h_ref[...]
    xn = _rms(x, g_ref[...]).astype(BF16)
    acc = jnp.zeros((tile, D_MODEL), F32)
    for j in range(D_FF // tf):
        halves = []
        for part in range(2):
            c0 = part * D_FF + j * tf
            slot = 2 * j + part
            pre = _dot(xn, wup_ref[:, c0:c0 + tf])
            halves.append(_causal_conv(pre, halo_ref.at[slot], xs_ref.at[part], convw_ref[:, c0:c0 + tf]))
        act = (halves[0] * _sigmoid(halves[0]) * halves[1]).astype(BF16)
        acc = acc + _dot(act, wdown_ref[j * tf:(j + 1) * tf, :])
    y = x + acc
    if final_norm:
        y = _rms(y, gf_ref[...])
    out_ref[...] = y


def _ffn(h, g, w_up, conv_ffn, w_down, g_final, layer, tm, seq, final_norm):
    n = h.shape[0]
    return pl.pallas_call(
        functools.partial(_ffn_kernel, tiles_per_seq=seq // tm, final_norm=final_norm),
        grid=(n // tm,),
        in_specs=[
            pl.BlockSpec((tm, D_MODEL), lambda i: (i, 0)),
            _resident((1, D_MODEL)),
            _layer_resident((D_MODEL, 2 * D_FF), layer),
            _resident((FFN_CONV, 2 * D_FF)),
            _layer_resident((D_FF, D_MODEL), layer),
            _resident((1, D_MODEL)),
        ],
        out_specs=pl.BlockSpec((tm, D_MODEL), lambda i: (i, 0)),
        out_shape=jax.ShapeDtypeStruct((n, D_MODEL), F32),
        scratch_shapes=[
            pltpu.VMEM((2 * (D_FF // FFN_TILE), ROW_HALO, FFN_TILE), F32),
            pltpu.VMEM((2, ROW_HALO + tm, FFN_TILE), F32),
        ],
        compiler_params=_params(("arbitrary",)),
        name="ffn",
    )(h, g, w_up, conv_ffn, w_down, g_final)


def _reorder_kernel(main_ref, next_ref, s1_ref, s2_ref, big_ref, small_ref):
    j = pl.program_id(1)
    cat = jnp.concatenate([main_ref[0], next_ref[0]], axis=1)

    def emit(shift):
        big_ref[0] = cat[:, shift:shift + D_MODEL].astype(BF16)

    pl.when(j < COL_BCH // D_MODEL)(lambda: emit(0))
    pl.when((j >= COL_BCH // D_MODEL) & (j < COL_GATES // D_MODEL))(lambda: emit(2 * N_HEADS))
    pl.when(j >= COL_GATES // D_MODEL)(lambda: emit(3 * N_HEADS))
    lane = lax.broadcasted_iota(jnp.int32, s1_ref.shape[1:], 1)
    small_ref[0] = jnp.where(lane < LANE_F, s1_ref[0], jnp.where(lane < LANE_F + N_HEADS, s2_ref[0], 0.0)).astype(BF16)


def _reorder_w_in(w_in):
    depth, d, width = w_in.shape
    scalars_1 = COL_BCH
    scalars_2 = COL_GATES + 2 * N_HEADS
    assert width == PROJ_COLS + 3 * N_HEADS and COL_BCH % D_MODEL == 0 and COL_GATES % D_MODEL == 0
    assert scalars_1 % SMALL_COLS == LANE_BETA and scalars_2 % SMALL_COLS == LANE_F and LANE_F == 2 * N_HEADS
    per_block = D_MODEL // SMALL_COLS
    return pl.pallas_call(
        _reorder_kernel,
        grid=(depth, PROJ_COLS // D_MODEL),
        in_specs=[
            pl.BlockSpec((1, d, D_MODEL), lambda l, j: (l, 0, j)),
            pl.BlockSpec((1, d, SMALL_COLS), lambda l, j: (l, 0, (j + 1) * per_block)),
            pl.BlockSpec((1, d, SMALL_COLS), lambda l, j: (l, 0, scalars_1 // SMALL_COLS)),
            pl.BlockSpec((1, d, SMALL_COLS), lambda l, j: (l, 0, scalars_2 // SMALL_COLS)),
        ],
        out_specs=[
            pl.BlockSpec((1, d, D_MODEL), lambda l, j: (l, 0, j)),
            pl.BlockSpec((1, d, SMALL_COLS), lambda l, j: (l, 0, 0)),
        ],
        out_shape=[
            jax.ShapeDtypeStruct((depth, d, PROJ_COLS), BF16),
            jax.ShapeDtypeStruct((depth, d, SMALL_COLS), BF16),
        ],
        compiler_params=_params(("arbitrary", "arbitrary")),
        name="reorder_w_in",
    )(w_in, w_in, w_in, w_in)


def _lane_row(pairs):
    row = jnp.zeros((1, SMALL_COLS), F32)
    for lane0, vals in pairs:
        row = row.at[0, lane0:lane0 + N_HEADS].set(vals.astype(F32))
    return row


def kernel(x, norm1_g, w_in, conv_qkv, a_log, dt_bias, gdn_norm, w_br_a, conv_sc, w_br_b, fox_bias, w_br_c, w_o, norm2_g, w_up, conv_ffn, w_down, norm_f):
    batch, seq, d = x.shape
    assert d == D_MODEL and seq % TOKEN_TILE == 0 and seq % GDN_TILE == 0
    tm = TOKEN_TILE
    h = x.reshape(batch * seq, d)
    w_big, w_small = _reorder_w_in(w_in)
    w_a, w_b, w_c, w_out = (w.astype(BF16) for w in (w_br_a, w_br_b, w_br_c, w_o))
    w_up16, w_down16 = w_up.astype(BF16), w_down.astype(BF16)
    for l in range(DEPTH):
        proj, small = _inproj(h, norm1_g[l][None, :], w_big, w_small, conv_qkv[l], l, tm, seq)
        bias_row = _lane_row([(LANE_G, dt_bias[l]), (LANE_F, fox_bias[l])])
        alog_row = _lane_row([(LANE_G, a_log[l])])
        gate = _gates(small, bias_row, alog_row, batch, seq)
        oa = _gdn(proj, gate, gdn_norm[l][None, :], batch, seq)
        oc = _fox(proj, gate, batch, seq)
        h = _merge(oa, oc, proj, h, conv_sc[l], w_a, w_b, w_c, w_out, l, tm, seq)
        h = _ffn(h, norm2_g[l][None, :], w_up16, conv_ffn[l], w_down16, norm_f[None, :], l, tm, seq,
                 final_norm=(l == DEPTH - 1))
    return h.reshape(batch, seq, d)
```

```python
import functools

import jax
import jax.numpy as jnp
from jax import lax
from jax.experimental import pallas as pl
from jax.experimental.pallas import tpu as pltpu

F32 = jnp.float32
BF16 = jnp.bfloat16

D_MODEL = 1024
DEPTH = 4
N_HEADS = 4
HEAD_DIM = 128
WIDTH = N_HEADS * HEAD_DIM
GDN_CONV = 4
GDN_CHUNK = 64
SC_CONV = 3
D_FF = 2816
FFN_CONV = 3
EPS = 1e-6

COL_QKV_A = 0
COL_Z_A = 3 * WIDTH
COL_BCH = 4 * WIDTH
COL_QKV_C = 7 * WIDTH
COL_GATES = 10 * WIDTH
PROJ_COLS = 10 * WIDTH + 3 * D_MODEL
LANE_BETA = 0
LANE_G = 4
LANE_F = 8
SMALL_COLS = 128

ROW_HALO = 8
TOKEN_TILE = 512
GDN_TILE = 256
GATES_TILE = 512
FOX_Q_TILE = 256
FOX_K_TILE = 256
FFN_TILE = 256
VMEM_LIMIT = 56 * 1024 * 1024


def _resident(shape):
    nd = len(shape)
    return pl.BlockSpec(shape, lambda *_: (0,) * nd, pipeline_mode=pl.Buffered(1))


def _layer_resident(shape, layer):
    nd = len(shape)
    return pl.BlockSpec((None,) + shape, lambda *_: (layer,) + (0,) * nd, pipeline_mode=pl.Buffered(1))


def _params(sem):
    return pltpu.CompilerParams(dimension_semantics=sem, vmem_limit_bytes=VMEM_LIMIT)


def _rms(x, g):
    return x * lax.rsqrt(jnp.mean(x * x, axis=-1, keepdims=True) + EPS) * g


def _sigmoid(x):
    return 1.0 / (1.0 + jnp.exp(-x))


def _dot(a, b):
    return jnp.dot(a, b, preferred_element_type=F32)


def _dot_nt(a, b):
    return lax.dot_general(a, b, (((1,), (1,)), ((), ())), preferred_element_type=F32)


def _dot_tn(a, b):
    return lax.dot_general(a, b, (((0,), (0,)), ((), ())), preferred_element_type=F32)


def _bdot(a, b):
    return lax.dot_general(a, b, (((2,), (1,)), ((0,), (0,))), preferred_element_type=F32)


def _bdot_nt(a, b):
    return lax.dot_general(a, b, (((2,), (2,)), ((0,), (0,))), preferred_element_type=F32)


def _split3(x):
    h1 = x.astype(BF16)
    r1 = x - h1.astype(F32)
    h2 = r1.astype(BF16)
    h3 = (r1 - h2.astype(F32)).astype(BF16)
    return h1, h2, h3


def _bdot_f32(a, b):
    a1 = a.astype(BF16)
    a2 = (a - a1.astype(F32)).astype(BF16)
    b1 = b.astype(BF16)
    b2 = (b - b1.astype(F32)).astype(BF16)
    return _bdot(a1, b1) + (_bdot(a1, b2) + _bdot(a2, b1))


def _diff_operands(c):
    shape = c.shape[:-1] + (HEAD_DIM,)
    lane = lax.broadcasted_iota(jnp.int32, shape, len(shape) - 1)
    c1, c2, c3 = (term.astype(F32) for term in _split3(jnp.broadcast_to(c, shape)))
    terms = jnp.where((lane == 0) | (lane == 3), c1, jnp.where((lane == 1) | (lane == 4), c2, c3))
    a = jnp.where(lane < 3, terms, jnp.where(lane < 6, 1.0, 0.0))
    b = jnp.where(lane < 3, 1.0, jnp.where(lane < 6, -terms, 0.0))
    return a.astype(BF16), b.astype(BF16)


def _conv_stage(pre, halo_ref, xs_ref):
    tile = pre.shape[0]
    xs_ref[0:ROW_HALO, :] = halo_ref[...]
    xs_ref[ROW_HALO:ROW_HALO + tile, :] = pre
    halo_ref[...] = pre[tile - ROW_HALO:tile, :]


def _conv_apply(xs_ref, taps):
    ntap = taps.shape[0]
    staged = xs_ref[...]
    out = None
    for k in range(ntap):
        shift = ntap - 1 - k
        rows = staged if shift == 0 else pltpu.roll(staged, shift, axis=0)
        term = rows[ROW_HALO:, :] * taps[k:k + 1, :]
        out = term if out is None else out + term
    return out


def _inproj_kernel(x_ref, g_ref, w_ref, ws_ref, convw_ref, proj_ref, small_ref, halo_ref, xs_ref,
                   *, tiles_per_seq):
    @pl.when(pl.program_id(0) % tiles_per_seq == 0)
    def _():
        halo_ref[...] = jnp.zeros_like(halo_ref)

    xn = _rms(x_ref[...], g_ref[...]).astype(BF16)

    for part in range(3):
        c0 = COL_QKV_A + part * WIDTH
        _conv_stage(_dot(xn, w_ref[:, c0:c0 + WIDTH]), halo_ref.at[part], xs_ref.at[part])

    def qkv_epilogue(part):
        c0 = COL_QKV_A + part * WIDTH
        y = _conv_apply(xs_ref.at[part], convw_ref[:, c0:c0 + WIDTH])
        y = y * _sigmoid(y)
        for h in range(N_HEADS):
            lo = h * HEAD_DIM
            yh = y[:, lo:lo + HEAD_DIM]
            if part < 2:
                yh = yh * lax.rsqrt(jnp.sum(yh * yh, axis=-1, keepdims=True) + EPS)
            if part == 0:
                yh = yh * (HEAD_DIM ** -0.5)
            proj_ref[:, c0 + lo:c0 + lo + HEAD_DIM] = yh.astype(BF16)

    plain = [(COL_Z_A, COL_BCH)] + [(c0, c0 + D_MODEL) for c0 in range(COL_BCH, PROJ_COLS, D_MODEL)]
    for idx, (c0, c1) in enumerate(plain):
        proj_ref[:, c0:c1] = _dot(xn, w_ref[:, c0:c1]).astype(BF16)
        if idx % 2 == 0 and idx // 2 < 3:
            qkv_epilogue(idx // 2)
    small_ref[...] = _dot(xn, ws_ref[...])


def _inproj(h, g, w_big, w_small, conv_qkv, layer, tm, seq):
    n = h.shape[0]
    return pl.pallas_call(
        functools.partial(_inproj_kernel, tiles_per_seq=seq // tm),
        grid=(n // tm,),
        in_specs=[
            pl.BlockSpec((tm, D_MODEL), lambda i: (i, 0)),
            _resident((1, D_MODEL)),
            _layer_resident((D_MODEL, PROJ_COLS), layer),
            _layer_resident((D_MODEL, SMALL_COLS), layer),
            _resident((GDN_CONV, 3 * WIDTH)),
        ],
        out_specs=[
            pl.BlockSpec((tm, PROJ_COLS), lambda i: (i, 0)),
            pl.BlockSpec((tm, SMALL_COLS), lambda i: (i, 0)),
        ],
        out_shape=[
            jax.ShapeDtypeStruct((n, PROJ_COLS), BF16),
            jax.ShapeDtypeStruct((n, SMALL_COLS), F32),
        ],
        scratch_shapes=[
            pltpu.VMEM((3, ROW_HALO, WIDTH), F32),
            pltpu.VMEM((3, ROW_HALO + tm, WIDTH), F32),
        ],
        compiler_params=_params(("arbitrary",)),
        name="inproj",
    )(h, g, w_big, w_small, conv_qkv)


def _gates_kernel(small_ref, bias_ref, alog_ref, out_ref, carry_ref):
    t = pl.program_id(1)
    tile = small_ref.shape[0]

    @pl.when(t == 0)
    def _():
        carry_ref[...] = jnp.zeros_like(carry_ref)

    x = small_ref[...] + bias_ref[...]
    lane = lax.broadcasted_iota(jnp.int32, x.shape, 1)
    soft = jnp.log1p(jnp.exp(-jnp.abs(x)))
    beta = _sigmoid(x)
    g = -jnp.exp(alog_ref[...]) * (jnp.maximum(x, 0.0) + soft)
    logf = jnp.minimum(x, 0.0) - soft
    vals = jnp.where(lane < LANE_G, beta, jnp.where(lane < LANE_F, g, logf))

    row = lax.broadcasted_iota(jnp.int32, (tile, tile), 0)
    col = lax.broadcasted_iota(jnp.int32, (tile, tile), 1)
    tril = row >= col
    m_full = tril.astype(BF16)
    m_seg = (tril & (row // GDN_CHUNK == col // GDN_CHUNK)).astype(BF16)
    v1, v2, v3 = _split3(vals)
    cum_full = _dot(m_full, v1) + (_dot(m_full, v2) + _dot(m_full, v3)) + carry_ref[0:1, :]
    cum_seg = _dot(m_seg, v1) + (_dot(m_seg, v2) + _dot(m_seg, v3))
    carry_ref[0:1, :] = cum_full[tile - 1:tile, :]
    out_ref[...] = jnp.where(lane < LANE_G, beta, jnp.where(lane < LANE_F, cum_seg, cum_full))


def _gates(small, bias_row, alog_row, batch, seq):
    tile = GATES_TILE
    nt = seq // tile
    return pl.pallas_call(
        _gates_kernel,
        grid=(batch, nt),
        in_specs=[
            pl.BlockSpec((tile, SMALL_COLS), lambda b, t: (b * nt + t, 0)),
            _resident((1, SMALL_COLS)),
            _resident((1, SMALL_COLS)),
        ],
        out_specs=pl.BlockSpec((tile, SMALL_COLS), lambda b, t: (b * nt + t, 0)),
        out_shape=jax.ShapeDtypeStruct(small.shape, F32),
        scratch_shapes=[pltpu.VMEM((ROW_HALO, SMALL_COLS), F32)],
        compiler_params=_params(("arbitrary", "arbitrary")),
        name="gates",
    )(small, bias_row, alog_row)


def _unit_lower_inverse(l_strict):
    c = l_strict.shape[-1]
    row = lax.broadcasted_iota(jnp.int32, (c, c), 0)
    col = lax.broadcasted_iota(jnp.int32, (c, c), 1)
    power = -l_strict
    prod = jnp.where(row == col, 1.0, 0.0)[None] + power
    span = 1
    while 2 * span < c:
        power = _bdot_f32(power, power) if span == 1 else stacked[:, :c]
        span *= 2
        if 2 * span < c:
            stacked = _bdot_f32(jnp.concatenate([power, prod], axis=1), power)
            prod = prod + stacked[:, c:]
        else:
            prod = prod + _bdot_f32(prod, power)
    return prod


def _gdn_kernel(qkv_ref, z_ref, gate_ref, norm_ref, o_ref, state_ref):
    tile = qkv_ref.shape[0]

    @pl.when(pl.program_id(1) == 0)
    def _():
        state_ref[...] = jnp.zeros_like(state_ref)

    c = GDN_CHUNK
    nchunk = tile // c
    problems = [(j, h) for j in range(nchunk) for h in range(N_HEADS)]

    def slabs(col0):
        return jnp.stack([qkv_ref[j * c:(j + 1) * c, col0 + h * HEAD_DIM:col0 + (h + 1) * HEAD_DIM]
                          for j, h in problems])

    def lanes(lane0, rows=None):
        return jnp.stack([gate_ref[(j * c if rows is None else j * c + rows):(j + 1) * c,
                                   lane0 + h:lane0 + h + 1] for j, h in problems])

    q = slabs(0).astype(F32)
    k16 = slabs(WIDTH)
    k = k16.astype(F32)
    v = slabs(2 * WIDTH).astype(F32)
    beta = lanes(LANE_BETA)
    gc = lanes(LANE_G)
    g_last = lanes(LANE_G, rows=c - 1)

    row = lax.broadcasted_iota(jnp.int32, (c, c), 0)
    col = lax.broadcasted_iota(jnp.int32, (c, c), 1)
    da, db = _diff_operands(gc)
    decay = jnp.exp(jnp.where((row >= col)[None], _bdot_nt(da, db), -jnp.inf))
    kb = k * beta
    l_strict = jnp.where((row > col)[None], _bdot_nt(kb.astype(BF16), k16) * decay, 0.0)
    attn = (_bdot_nt(slabs(0), k16) * decay).astype(BF16)
    tm = _unit_lower_inverse(l_strict).astype(BF16)
    u = _bdot(tm, (v * beta).astype(BF16))
    w = _bdot(tm, (kb * jnp.exp(gc)).astype(BF16)).astype(BF16)
    qg = (q * jnp.exp(gc)).astype(BF16)
    kd = (k * jnp.exp(g_last - gc)).astype(BF16)
    s_decay = jnp.exp(g_last)

    u_hi = u.astype(BF16)
    u_lo = (u - u_hi.astype(F32)).astype(BF16)
    wuu = jnp.concatenate([w, u_hi, u_lo], axis=-1)
    kd_t = jnp.stack([_dot_tn(kd[p], wuu[p]) for p in range(len(problems))])
    kd_w = kd_t[..., 0:HEAD_DIM]
    kd_u = kd_t[..., HEAD_DIM:2 * HEAD_DIM] + kd_t[..., 2 * HEAD_DIM:3 * HEAD_DIM]
    kd_w_hi = kd_w.astype(BF16)
    kd_w_split = jnp.concatenate([kd_w_hi, (kd_w - kd_w_hi.astype(F32)).astype(BF16)], axis=-1)

    s = state_ref[...]
    starts = []
    for j in range(nchunk):
        b0, b1 = j * N_HEADS, (j + 1) * N_HEADS
        s16 = s.astype(BF16)
        starts.append(s16)
        s = s * s_decay[b0:b1] + kd_u[b0:b1] - _bdot(kd_w_split[b0:b1], jnp.concatenate([s16, s16], axis=1))
    state_ref[...] = s

    s_start = jnp.concatenate(starts, axis=0)
    v_new = (u - _bdot(w, s_start)).astype(BF16)
    o = _rms(_bdot(qg, s_start) + _bdot(attn, v_new), norm_ref[...][None])
    for p, (j, h) in enumerate(problems):
        lo = h * HEAD_DIM
        z = z_ref[j * c:(j + 1) * c, lo:lo + HEAD_DIM].astype(F32)
        o_ref[j * c:(j + 1) * c, lo:lo + HEAD_DIM] = (o[p] * (z * _sigmoid(z))).astype(BF16)


def _gdn(proj, gate, gdn_norm, batch, seq):
    tile = GDN_TILE
    nt = seq // tile
    n = proj.shape[0]
    return pl.pallas_call(
        _gdn_kernel,
        grid=(batch, nt),
        in_specs=[
            pl.BlockSpec((tile, 3 * WIDTH), lambda b, t: (b * nt + t, COL_QKV_A // (3 * WIDTH))),
            pl.BlockSpec((tile, WIDTH), lambda b, t: (b * nt + t, COL_Z_A // WIDTH)),
            pl.BlockSpec((tile, SMALL_COLS), lambda b, t: (b * nt + t, 0)),
            _resident((1, HEAD_DIM)),
        ],
        out_specs=pl.BlockSpec((tile, WIDTH), lambda b, t: (b * nt + t, 0)),
        out_shape=jax.ShapeDtypeStruct((n, WIDTH), BF16),
        scratch_shapes=[pltpu.VMEM((N_HEADS, HEAD_DIM, HEAD_DIM), F32)],
        compiler_params=_params(("arbitrary", "arbitrary")),
        name="gdn",
    )(proj, proj, gate, gdn_norm)


def _fox_kernel(q_ref, k_ref, v_ref, gate_ref, o_ref, qa_ref, ka_ref):
    seq = q_ref.shape[0]
    tq, tk = FOX_Q_TILE, FOX_K_TILE
    scale = HEAD_DIM ** -0.5
    lane = lax.broadcasted_iota(jnp.int32, (seq, SMALL_COLS), 1)
    c = jnp.sum(jnp.where(lane == LANE_F + pl.program_id(1), gate_ref[...], 0.0), axis=-1, keepdims=True)
    ca, cb = _diff_operands(c * (1.0 / scale))
    qa_ref[:, 0:HEAD_DIM] = q_ref[...]
    qa_ref[:, HEAD_DIM:2 * HEAD_DIM] = ca
    ka_ref[:, 0:HEAD_DIM] = k_ref[...]
    ka_ref[:, HEAD_DIM:2 * HEAD_DIM] = cb
    row = lax.broadcasted_iota(jnp.int32, (tq, tk), 0)
    col = lax.broadcasted_iota(jnp.int32, (tq, tk), 1)
    log2e_scale = scale * 1.4426950408889634
    n_q = seq // tq
    m = [jnp.full((tq, 1), -jnp.inf, F32) for _ in range(n_q)]
    den = [jnp.zeros((tq, 1), F32) for _ in range(n_q)]
    acc = [jnp.zeros((tq, HEAD_DIM), F32) for _ in range(n_q)]
    for j in range(seq // tk):
        kblk = ka_ref[j * tk:(j + 1) * tk, :]
        vblk = v_ref[j * tk:(j + 1) * tk, :]
        for i in range((j * tk) // tq, n_q):
            s = _dot_nt(qa_ref[i * tq:(i + 1) * tq, :], kblk)
            if (i * tq) // tk == j:
                s = jnp.where(row + (i * tq - j * tk) >= col, s, -jnp.inf)
            m_new = jnp.maximum(m[i], jnp.max(s, axis=-1, keepdims=True))
            p = jnp.exp2((s - m_new) * log2e_scale)
            alpha = jnp.exp2((m[i] - m_new) * log2e_scale)
            den[i] = alpha * den[i] + jnp.sum(p, axis=-1, keepdims=True)
            acc[i] = alpha * acc[i] + _dot(p.astype(BF16), vblk)
            m[i] = m_new
            if (i * tq) // tk == j:
                o_ref[i * tq:(i + 1) * tq, :] = (acc[i] / den[i]).astype(BF16)


def _fox(proj, gate, batch, seq):
    n = proj.shape[0]
    blk = COL_QKV_C // HEAD_DIM
    return pl.pallas_call(
        _fox_kernel,
        grid=(batch, N_HEADS),
        in_specs=[
            pl.BlockSpec((seq, HEAD_DIM), lambda b, h: (b, blk + h)),
            pl.BlockSpec((seq, HEAD_DIM), lambda b, h: (b, blk + N_HEADS + h)),
            pl.BlockSpec((seq, HEAD_DIM), lambda b, h: (b, blk + 2 * N_HEADS + h)),
            pl.BlockSpec((seq, SMALL_COLS), lambda b, h: (b, 0)),
        ],
        out_specs=pl.BlockSpec((seq, HEAD_DIM), lambda b, h: (b, h)),
        out_shape=jax.ShapeDtypeStruct((n, WIDTH), BF16),
        scratch_shapes=[
            pltpu.VMEM((seq, 2 * HEAD_DIM), BF16),
            pltpu.VMEM((seq, 2 * HEAD_DIM), BF16),
        ],
        compiler_params=_params(("arbitrary", "arbitrary")),
        name="fox",
    )(proj, proj, proj, gate)


def _merge_kernel(oa_ref, oc_ref, bg_ref, cg_ref, hh_ref, ga_ref, gb_ref, gc_ref, h_ref,
                  convw_ref, wa_ref, wb_ref, wc_ref, wo_ref, out_ref, halo_ref, xs_ref,
                  *, tiles_per_seq):
    tile = h_ref.shape[0]

    @pl.when(pl.program_id(0) % tiles_per_seq == 0)
    def _():
        halo_ref[...] = jnp.zeros_like(halo_ref)

    _conv_stage(cg_ref[...].astype(F32) * hh_ref[...].astype(F32), halo_ref, xs_ref)
    conv = _conv_apply(xs_ref, convw_ref[...])
    sc = (bg_ref[...].astype(F32) * conv).astype(BF16)

    mix = _sigmoid(ga_ref[...].astype(F32)) * _dot(oa_ref[...], wa_ref[...])
    mix = mix + _sigmoid(gb_ref[...].astype(F32)) * _dot(sc, wb_ref[...])
    mix = mix + _sigmoid(gc_ref[...].astype(F32)) * _dot(oc_ref[...], wc_ref[...])
    out_ref[...] = h_ref[...] + _dot(mix.astype(BF16), wo_ref[...])


def _merge(oa, oc, proj, h, conv_sc, w_a, w_b, w_c, w_o, layer, tm, seq):
    n = h.shape[0]
    bch = COL_BCH // WIDTH
    gates = COL_GATES // D_MODEL
    tok = lambda width, blk: pl.BlockSpec((tm, width), lambda i: (i, blk))
    return pl.pallas_call(
        functools.partial(_merge_kernel, tiles_per_seq=seq // tm),
        grid=(n // tm,),
        in_specs=[
            tok(WIDTH, 0), tok(WIDTH, 0),
            tok(WIDTH, bch), tok(WIDTH, bch + 1), tok(WIDTH, bch + 2),
            tok(D_MODEL, gates), tok(D_MODEL, gates + 1), tok(D_MODEL, gates + 2),
            tok(D_MODEL, 0),
            _resident((SC_CONV, WIDTH)),
            _layer_resident((WIDTH, D_MODEL), layer), _layer_resident((WIDTH, D_MODEL), layer),
            _layer_resident((WIDTH, D_MODEL), layer), _layer_resident((D_MODEL, D_MODEL), layer),
        ],
        out_specs=tok(D_MODEL, 0),
        out_shape=jax.ShapeDtypeStruct((n, D_MODEL), F32),
        scratch_shapes=[
            pltpu.VMEM((ROW_HALO, WIDTH), F32),
            pltpu.VMEM((ROW_HALO + tm, WIDTH), F32),
        ],
        compiler_params=_params(("arbitrary",)),
        name="merge",
    )(oa, oc, proj, proj, proj, proj, proj, proj, h, conv_sc, w_a, w_b, w_c, w_o)


def _ffn_kernel(h_ref, g_ref, wup_ref, convw_ref, wdown_ref, gf_ref, out_ref, halo_ref, xs_ref, act_ref,
                *, tiles_per_seq, final_norm):
    tile = h_ref.shape[0]
    tf = FFN_TILE

    @pl.when(pl.program_id(0) % tiles_per_seq == 0)
    def _():
        halo_ref[...] = jnp.zeros_like(halo_ref)

    x = h_ref[...]
    xn = _rms(x, g_ref[...]).astype(BF16)
    def stage(j):
        for part in range(2):
            c0 = part * D_FF + j * tf
            _conv_stage(_dot(xn, wup_ref[:, c0:c0 + tf]), halo_ref.at[2 * j + part], xs_ref.at[2 * j + part])

    def finish(j):
        gate, up = (_conv_apply(xs_ref.at[2 * j + part], convw_ref[:, part * D_FF + j * tf:part * D_FF + (j + 1) * tf])
                    for part in range(2))
        act_ref[:, j * tf:(j + 1) * tf] = (gate * _sigmoid(gate) * up).astype(BF16)

    n_tiles = D_FF // tf
    stage(0)
    for j in range(n_tiles):
        if j + 1 < n_tiles:
            stage(j + 1)
        finish(j)
    y = x + _dot(act_ref[...], wdown_ref[...])
    if final_norm:
        y = _rms(y, gf_ref[...])
    out_ref[...] = y


def _ffn(h, g, w_up, conv_ffn, w_down, g_final, layer, tm, seq, final_norm):
    n = h.shape[0]
    return pl.pallas_call(
        functools.partial(_ffn_kernel, tiles_per_seq=seq // tm, final_norm=final_norm),
        grid=(n // tm,),
        in_specs=[
            pl.BlockSpec((tm, D_MODEL), lambda i: (i, 0)),
            _resident((1, D_MODEL)),
            _layer_resident((D_MODEL, 2 * D_FF), layer),
            _resident((FFN_CONV, 2 * D_FF)),
            _layer_resident((D_FF, D_MODEL), layer),
            _resident((1, D_MODEL)),
        ],
        out_specs=pl.BlockSpec((tm, D_MODEL), lambda i: (i, 0)),
        out_shape=jax.ShapeDtypeStruct((n, D_MODEL), F32),
        scratch_shapes=[
            pltpu.VMEM((2 * (D_FF // FFN_TILE), ROW_HALO, FFN_TILE), F32),
            pltpu.VMEM((2 * (D_FF // FFN_TILE), ROW_HALO + tm, FFN_TILE), F32),
            pltpu.VMEM((tm, D_FF), BF16),
        ],
        compiler_params=_params(("arbitrary",)),
        name="ffn",
    )(h, g, w_up, conv_ffn, w_down, g_final)


def _reorder_kernel(main_ref, next_ref, s1_ref, s2_ref, big_ref, small_ref):
    j = pl.program_id(1)
    cat = jnp.concatenate([main_ref[0], next_ref[0]], axis=1)

    def emit(shift):
        big_ref[0] = cat[:, shift:shift + D_MODEL].astype(BF16)

    pl.when(j < COL_BCH // D_MODEL)(lambda: emit(0))
    pl.when((j >= COL_BCH // D_MODEL) & (j < COL_GATES // D_MODEL))(lambda: emit(2 * N_HEADS))
    pl.when(j >= COL_GATES // D_MODEL)(lambda: emit(3 * N_HEADS))
    lane = lax.broadcasted_iota(jnp.int32, s1_ref.shape[1:], 1)
    small_ref[0] = jnp.where(lane < LANE_F, s1_ref[0], jnp.where(lane < LANE_F + N_HEADS, s2_ref[0], 0.0)).astype(BF16)


def _reorder_w_in(w_in):
    depth, d, width = w_in.shape
    scalars_1 = COL_BCH
    scalars_2 = COL_GATES + 2 * N_HEADS
    assert width == PROJ_COLS + 3 * N_HEADS and COL_BCH % D_MODEL == 0 and COL_GATES % D_MODEL == 0
    assert scalars_1 % SMALL_COLS == LANE_BETA and scalars_2 % SMALL_COLS == LANE_F and LANE_F == 2 * N_HEADS
    per_block = D_MODEL // SMALL_COLS
    return pl.pallas_call(
        _reorder_kernel,
        grid=(depth, PROJ_COLS // D_MODEL),
        in_specs=[
            pl.BlockSpec((1, d, D_MODEL), lambda l, j: (l, 0, j)),
            pl.BlockSpec((1, d, SMALL_COLS), lambda l, j: (l, 0, (j + 1) * per_block)),
            pl.BlockSpec((1, d, SMALL_COLS), lambda l, j: (l, 0, scalars_1 // SMALL_COLS)),
            pl.BlockSpec((1, d, SMALL_COLS), lambda l, j: (l, 0, scalars_2 // SMALL_COLS)),
        ],
        out_specs=[
            pl.BlockSpec((1, d, D_MODEL), lambda l, j: (l, 0, j)),
            pl.BlockSpec((1, d, SMALL_COLS), lambda l, j: (l, 0, 0)),
        ],
        out_shape=[
            jax.ShapeDtypeStruct((depth, d, PROJ_COLS), BF16),
            jax.ShapeDtypeStruct((depth, d, SMALL_COLS), BF16),
        ],
        compiler_params=_params(("arbitrary", "arbitrary")),
        name="reorder_w_in",
    )(w_in, w_in, w_in, w_in)


def _lane_row(pairs):
    row = jnp.zeros((1, SMALL_COLS), F32)
    for lane0, vals in pairs:
        row = row.at[0, lane0:lane0 + N_HEADS].set(vals.astype(F32))
    return row


def kernel(x, norm1_g, w_in, conv_qkv, a_log, dt_bias, gdn_norm, w_br_a, conv_sc, w_br_b, fox_bias, w_br_c, w_o, norm2_g, w_up, conv_ffn, w_down, norm_f):
    batch, seq, d = x.shape
    assert d == D_MODEL and seq % TOKEN_TILE == 0 and seq % GATES_TILE == 0 and seq % FOX_K_TILE == 0
    tm = TOKEN_TILE
    h = x.reshape(batch * seq, d)
    w_big, w_small = _reorder_w_in(w_in)
    w_a, w_b, w_c, w_out = (w.astype(BF16) for w in (w_br_a, w_br_b, w_br_c, w_o))
    w_up16, w_down16 = w_up.astype(BF16), w_down.astype(BF16)
    for l in range(DEPTH):
        proj, small = _inproj(h, norm1_g[l][None, :], w_big, w_small, conv_qkv[l], l, tm, seq)
        bias_row = _lane_row([(LANE_G, dt_bias[l]), (LANE_F, fox_bias[l])])
        alog_row = _lane_row([(LANE_G, a_log[l])])
        gate = _gates(small, bias_row, alog_row, batch, seq)
        oa = _gdn(proj, gate, gdn_norm[l][None, :], batch, seq)
        oc = _fox(proj, gate, batch, seq)
        h = _merge(oa, oc, proj, h, conv_sc[l], w_a, w_b, w_c, w_out, l, tm, seq)
        h = _ffn(h, norm2_g[l][None, :], w_up16, conv_ffn[l], w_down16, norm_f[None, :], l, tm, seq,
                 final_norm=(l == DEPTH - 1))
    return h.reshape(batch, seq, d)
```

```python
import functools

import jax
import jax.numpy as jnp
from jax import lax
from jax.experimental import pallas as pl
from jax.experimental.pallas import tpu as pltpu

F32 = jnp.float32
BF16 = jnp.bfloat16

D_MODEL = 1024
DEPTH = 4
N_HEADS = 4
HEAD_DIM = 128
WIDTH = N_HEADS * HEAD_DIM
GDN_CONV = 4
GDN_CHUNK = 64
SC_CONV = 3
D_FF = 2816
FFN_CONV = 3
EPS = 1e-6

COL_QKV_A = 0
COL_Z_A = 3 * WIDTH
COL_BCH = 4 * WIDTH
COL_QKV_C = 7 * WIDTH
COL_GATES = 10 * WIDTH
PROJ_COLS = 10 * WIDTH + 3 * D_MODEL
LANE_BETA = 0
LANE_G = 4
LANE_F = 8
SMALL_COLS = 128

ROW_HALO = 8
TOKEN_TILE = 512
GDN_TILE = 256
GATES_TILE = 512
QKV_TILE = 256
FOX_Q_TILE = 256
FOX_K_TILE = 256
FFN_TILE = 256
VMEM_LIMIT = 56 * 1024 * 1024


def _resident(shape):
    nd = len(shape)
    return pl.BlockSpec(shape, lambda *_: (0,) * nd, pipeline_mode=pl.Buffered(1))


def _layer_resident(shape, layer):
    nd = len(shape)
    return pl.BlockSpec((None,) + shape, lambda *_: (layer,) + (0,) * nd, pipeline_mode=pl.Buffered(1))


def _params(sem):
    return pltpu.CompilerParams(dimension_semantics=sem, vmem_limit_bytes=VMEM_LIMIT)


def _rms(x, g):
    return x * lax.rsqrt(jnp.mean(x * x, axis=-1, keepdims=True) + EPS) * g


def _sigmoid(x):
    return 1.0 / (1.0 + jnp.exp(-x))


def _dot(a, b):
    return jnp.dot(a, b, preferred_element_type=F32)


def _dot_nt(a, b):
    return lax.dot_general(a, b, (((1,), (1,)), ((), ())), preferred_element_type=F32)


def _dot_tn(a, b):
    return lax.dot_general(a, b, (((0,), (0,)), ((), ())), preferred_element_type=F32)


def _bdot(a, b):
    return lax.dot_general(a, b, (((2,), (1,)), ((0,), (0,))), preferred_element_type=F32)


def _bdot_nt(a, b):
    return lax.dot_general(a, b, (((2,), (2,)), ((0,), (0,))), preferred_element_type=F32)


def _split3(x):
    h1 = x.astype(BF16)
    r1 = x - h1.astype(F32)
    h2 = r1.astype(BF16)
    h3 = (r1 - h2.astype(F32)).astype(BF16)
    return h1, h2, h3


def _bdot_f32(a, b):
    a_hi = a.astype(BF16).astype(F32)
    b_hi = b.astype(BF16).astype(F32)
    lhs = jnp.concatenate([a_hi, a - a_hi, a_hi], axis=-1).astype(BF16)
    rhs = jnp.concatenate([b_hi, b_hi, b - b_hi], axis=1).astype(BF16)
    return _bdot(lhs, rhs)


def _diff_operands(c):
    shape = c.shape[:-1] + (HEAD_DIM,)
    lane = lax.broadcasted_iota(jnp.int32, shape, len(shape) - 1)
    c1, c2, c3 = (term.astype(F32) for term in _split3(jnp.broadcast_to(c, shape)))
    terms = jnp.where((lane == 0) | (lane == 3), c1, jnp.where((lane == 1) | (lane == 4), c2, c3))
    a = jnp.where(lane < 3, terms, jnp.where(lane < 6, 1.0, 0.0))
    b = jnp.where(lane < 3, 1.0, jnp.where(lane < 6, -terms, 0.0))
    return a.astype(BF16), b.astype(BF16)


def _conv_stage(pre, halo_ref, xs_ref):
    tile = pre.shape[0]
    xs_ref[0:ROW_HALO, :] = halo_ref[...]
    xs_ref[ROW_HALO:ROW_HALO + tile, :] = pre
    halo_ref[...] = pre[tile - ROW_HALO:tile, :]


def _conv_apply(xs_ref, taps):
    ntap = taps.shape[0]
    staged = xs_ref[...]
    out = None
    for k in range(ntap):
        shift = ntap - 1 - k
        rows = staged if shift == 0 else pltpu.roll(staged, shift, axis=0)
        term = rows[ROW_HALO:, :] * taps[k:k + 1, :]
        out = term if out is None else out + term
    return out


def _inproj_kernel(x_ref, g_ref, w_ref, ws_ref, convw_ref, proj_ref, small_ref, halo_ref, xs_ref,
                   *, tiles_per_seq):
    @pl.when(pl.program_id(0) % tiles_per_seq == 0)
    def _():
        halo_ref[...] = jnp.zeros_like(halo_ref)

    xn = _rms(x_ref[...], g_ref[...]).astype(BF16)

    def stage(t):
        c0 = COL_QKV_A + t * QKV_TILE
        _conv_stage(_dot(xn, w_ref[:, c0:c0 + QKV_TILE]), halo_ref.at[t], xs_ref.at[t])

    def epilogue(t):
        c0 = COL_QKV_A + t * QKV_TILE
        part = (t * QKV_TILE) // WIDTH
        y = _conv_apply(xs_ref.at[t], convw_ref[:, c0:c0 + QKV_TILE])
        y = y * _sigmoid(y)
        for lo in range(0, QKV_TILE, HEAD_DIM):
            yh = y[:, lo:lo + HEAD_DIM]
            if part < 2:
                yh = yh * lax.rsqrt(jnp.sum(yh * yh, axis=-1, keepdims=True) + EPS)
            if part == 0:
                yh = yh * (HEAD_DIM ** -0.5)
            proj_ref[:, c0 + lo:c0 + lo + HEAD_DIM] = yh.astype(BF16)

    plain = [(COL_Z_A, COL_BCH)] + [(c0, c0 + D_MODEL) for c0 in range(COL_BCH, PROJ_COLS, D_MODEL)]
    n_qkv = 3 * WIDTH // QKV_TILE
    stage(0)
    for t in range(max(n_qkv, len(plain))):
        if t + 1 < n_qkv:
            stage(t + 1)
        if t < len(plain):
            c0, c1 = plain[t]
            proj_ref[:, c0:c1] = _dot(xn, w_ref[:, c0:c1]).astype(BF16)
        if t < n_qkv:
            epilogue(t)
    small_ref[...] = _dot(xn, ws_ref[...])


def _inproj(h, g, w_big, w_small, conv_qkv, layer, tm, seq):
    n = h.shape[0]
    return pl.pallas_call(
        functools.partial(_inproj_kernel, tiles_per_seq=seq // tm),
        grid=(n // tm,),
        in_specs=[
            pl.BlockSpec((tm, D_MODEL), lambda i: (i, 0)),
            _resident((1, D_MODEL)),
            _layer_resident((D_MODEL, PROJ_COLS), layer),
            _layer_resident((D_MODEL, SMALL_COLS), layer),
            _resident((GDN_CONV, 3 * WIDTH)),
        ],
        out_specs=[
            pl.BlockSpec((tm, PROJ_COLS), lambda i: (i, 0)),
            pl.BlockSpec((tm, SMALL_COLS), lambda i: (i, 0)),
        ],
        out_shape=[
            jax.ShapeDtypeStruct((n, PROJ_COLS), BF16),
            jax.ShapeDtypeStruct((n, SMALL_COLS), F32),
        ],
        scratch_shapes=[
            pltpu.VMEM((3 * WIDTH // QKV_TILE, ROW_HALO, QKV_TILE), F32),
            pltpu.VMEM((3 * WIDTH // QKV_TILE, ROW_HALO + tm, QKV_TILE), F32),
        ],
        compiler_params=_params(("arbitrary",)),
        name="inproj",
    )(h, g, w_big, w_small, conv_qkv)


def _gates_kernel(small_ref, bias_ref, alog_ref, out_ref, carry_ref):
    t = pl.program_id(1)
    tile = small_ref.shape[0]

    @pl.when(t == 0)
    def _():
        carry_ref[...] = jnp.zeros_like(carry_ref)

    x = small_ref[...] + bias_ref[...]
    lane = lax.broadcasted_iota(jnp.int32, x.shape, 1)
    soft = jnp.log1p(jnp.exp(-jnp.abs(x)))
    beta = _sigmoid(x)
    g = -jnp.exp(alog_ref[...]) * (jnp.maximum(x, 0.0) + soft)
    logf = jnp.minimum(x, 0.0) - soft
    vals = jnp.where(lane < LANE_G, beta, jnp.where(lane < LANE_F, g, logf))

    row = lax.broadcasted_iota(jnp.int32, (tile, tile), 0)
    col = lax.broadcasted_iota(jnp.int32, (tile, tile), 1)
    tril = row >= col
    m_full = tril.astype(BF16)
    m_seg = (tril & (row // GDN_CHUNK == col // GDN_CHUNK)).astype(BF16)
    cum = _dot(jnp.concatenate([m_full, m_seg], axis=0), jnp.concatenate(_split3(vals), axis=1))
    cum = cum[:, 0:SMALL_COLS] + (cum[:, SMALL_COLS:2 * SMALL_COLS] + cum[:, 2 * SMALL_COLS:3 * SMALL_COLS])
    cum_full = cum[0:tile] + carry_ref[0:1, :]
    cum_seg = cum[tile:2 * tile]
    carry_ref[0:1, :] = cum_full[tile - 1:tile, :]
    out_ref[...] = jnp.where(lane < LANE_G, beta, jnp.where(lane < LANE_F, cum_seg, cum_full))


def _gates(small, bias_row, alog_row, batch, seq):
    tile = GATES_TILE
    nt = seq // tile
    return pl.pallas_call(
        _gates_kernel,
        grid=(batch, nt),
        in_specs=[
            pl.BlockSpec((tile, SMALL_COLS), lambda b, t: (b * nt + t, 0)),
            _resident((1, SMALL_COLS)),
            _resident((1, SMALL_COLS)),
        ],
        out_specs=pl.BlockSpec((tile, SMALL_COLS), lambda b, t: (b * nt + t, 0)),
        out_shape=jax.ShapeDtypeStruct(small.shape, F32),
        scratch_shapes=[pltpu.VMEM((ROW_HALO, SMALL_COLS), F32)],
        compiler_params=_params(("arbitrary", "arbitrary")),
        name="gates",
    )(small, bias_row, alog_row)


def _unit_lower_inverse(l_strict):
    c = l_strict.shape[-1]
    row = lax.broadcasted_iota(jnp.int32, (c, c), 0)
    col = lax.broadcasted_iota(jnp.int32, (c, c), 1)
    power = -l_strict
    prod = jnp.where(row == col, 1.0, 0.0)[None] + power
    span = 1
    while 2 * span < c:
        power = _bdot_f32(power, power) if span == 1 else stacked[:, :c]
        span *= 2
        if 2 * span < c:
            stacked = _bdot_f32(jnp.concatenate([power, prod], axis=1), power)
            prod = prod + stacked[:, c:]
        else:
            prod = prod + _bdot_f32(prod, power)
    return prod


def _gdn_kernel(qkv_ref, z_ref, gate_ref, norm_ref, o_ref, state_ref):
    tile = qkv_ref.shape[0]

    @pl.when(pl.program_id(1) == 0)
    def _():
        state_ref[...] = jnp.zeros_like(state_ref)

    c = GDN_CHUNK
    nchunk = tile // c
    problems = [(j, h) for j in range(nchunk) for h in range(N_HEADS)]

    def slabs(col0):
        return jnp.stack([qkv_ref[j * c:(j + 1) * c, col0 + h * HEAD_DIM:col0 + (h + 1) * HEAD_DIM]
                          for j, h in problems])

    def lanes(lane0, rows=None):
        return jnp.stack([gate_ref[(j * c if rows is None else j * c + rows):(j + 1) * c,
                                   lane0 + h:lane0 + h + 1] for j, h in problems])

    q = slabs(0).astype(F32)
    k16 = slabs(WIDTH)
    k = k16.astype(F32)
    v = slabs(2 * WIDTH).astype(F32)
    beta = lanes(LANE_BETA)
    gc = lanes(LANE_G)
    g_last = lanes(LANE_G, rows=c - 1)

    row = lax.broadcasted_iota(jnp.int32, (c, c), 0)
    col = lax.broadcasted_iota(jnp.int32, (c, c), 1)
    da, db = _diff_operands(gc)
    decay = jnp.exp(jnp.where((row >= col)[None], _bdot_nt(da, db), -jnp.inf))
    kb = k * beta
    kk_qk = _bdot_nt(jnp.concatenate([kb.astype(BF16), slabs(0)], axis=1), k16)
    l_strict = jnp.where((row > col)[None], kk_qk[:, 0:c] * decay, 0.0)
    attn = (kk_qk[:, c:2 * c] * decay).astype(BF16)
    tm = _unit_lower_inverse(l_strict).astype(BF16)
    uw = _bdot(tm, jnp.concatenate([v * beta, kb * jnp.exp(gc)], axis=-1).astype(BF16))
    u = uw[..., 0:HEAD_DIM]
    w = uw[..., HEAD_DIM:2 * HEAD_DIM].astype(BF16)
    qg = (q * jnp.exp(gc)).astype(BF16)
    kd = (k * jnp.exp(g_last - gc)).astype(BF16)
    s_decay = jnp.exp(g_last)

    u_hi = u.astype(BF16)
    u_lo = (u - u_hi.astype(F32)).astype(BF16)
    wuu = jnp.concatenate([w, u_hi, u_lo], axis=-1)
    kd_t = jnp.stack([_dot_tn(kd[p], wuu[p]) for p in range(len(problems))])
    kd_w = kd_t[..., 0:HEAD_DIM]
    kd_u = kd_t[..., HEAD_DIM:2 * HEAD_DIM] + kd_t[..., 2 * HEAD_DIM:3 * HEAD_DIM]
    kd_w_hi = kd_w.astype(BF16)
    kd_w_split = jnp.concatenate([kd_w_hi, (kd_w - kd_w_hi.astype(F32)).astype(BF16)], axis=-1)

    s = state_ref[...]
    starts = []
    for j in range(nchunk):
        b0, b1 = j * N_HEADS, (j + 1) * N_HEADS
        s16 = s.astype(BF16)
        starts.append(s16)
        s = s * s_decay[b0:b1] + kd_u[b0:b1] - _bdot(kd_w_split[b0:b1], jnp.concatenate([s16, s16], axis=1))
    state_ref[...] = s

    s_start = jnp.concatenate(starts, axis=0)
    wq_s = _bdot(jnp.concatenate([w, qg], axis=1), s_start)
    v_new = (u - wq_s[:, 0:c]).astype(BF16)
    o = _rms(wq_s[:, c:2 * c] + _bdot(attn, v_new), norm_ref[...][None])
    for p, (j, h) in enumerate(problems):
        lo = h * HEAD_DIM
        z = z_ref[j * c:(j + 1) * c, lo:lo + HEAD_DIM].astype(F32)
        o_ref[j * c:(j + 1) * c, lo:lo + HEAD_DIM] = (o[p] * (z * _sigmoid(z))).astype(BF16)


def _gdn(proj, gate, gdn_norm, batch, seq):
    tile = GDN_TILE
    nt = seq // tile
    n = proj.shape[0]
    return pl.pallas_call(
        _gdn_kernel,
        grid=(batch, nt),
        in_specs=[
            pl.BlockSpec((tile, 3 * WIDTH), lambda b, t: (b * nt + t, COL_QKV_A // (3 * WIDTH))),
            pl.BlockSpec((tile, WIDTH), lambda b, t: (b * nt + t, COL_Z_A // WIDTH)),
            pl.BlockSpec((tile, SMALL_COLS), lambda b, t: (b * nt + t, 0)),
            _resident((1, HEAD_DIM)),
        ],
        out_specs=pl.BlockSpec((tile, WIDTH), lambda b, t: (b * nt + t, 0)),
        out_shape=jax.ShapeDtypeStruct((n, WIDTH), BF16),
        scratch_shapes=[pltpu.VMEM((N_HEADS, HEAD_DIM, HEAD_DIM), F32)],
        compiler_params=_params(("arbitrary", "arbitrary")),
        name="gdn",
    )(proj, proj, gate, gdn_norm)


def _fox_kernel(q_ref, k_ref, v_ref, gate_ref, o_ref, qa_ref, ka_ref):
    seq = q_ref.shape[0]
    tq, tk = FOX_Q_TILE, FOX_K_TILE
    scale = HEAD_DIM ** -0.5
    lane = lax.broadcasted_iota(jnp.int32, (seq, SMALL_COLS), 1)
    c = jnp.sum(jnp.where(lane == LANE_F + pl.program_id(1), gate_ref[...], 0.0), axis=-1, keepdims=True)
    ca, cb = _diff_operands(c * (1.0 / scale))
    qa_ref[:, 0:HEAD_DIM] = q_ref[...]
    qa_ref[:, HEAD_DIM:2 * HEAD_DIM] = ca
    ka_ref[:, 0:HEAD_DIM] = k_ref[...]
    ka_ref[:, HEAD_DIM:2 * HEAD_DIM] = cb
    row = lax.broadcasted_iota(jnp.int32, (tq, tk), 0)
    col = lax.broadcasted_iota(jnp.int32, (tq, tk), 1)
    log2e_scale = scale * 1.4426950408889634
    n_q = seq // tq
    m = [jnp.full((tq, 1), -jnp.inf, F32) for _ in range(n_q)]
    den = [jnp.zeros((tq, 1), F32) for _ in range(n_q)]
    acc = [jnp.zeros((tq, HEAD_DIM), F32) for _ in range(n_q)]
    for j in range(seq // tk):
        kblk = ka_ref[j * tk:(j + 1) * tk, :]
        vblk = v_ref[j * tk:(j + 1) * tk, :]
        for i in range((j * tk) // tq, n_q):
            s = _dot_nt(qa_ref[i * tq:(i + 1) * tq, :], kblk)
            if (i * tq) // tk == j:
                s = jnp.where(row + (i * tq - j * tk) >= col, s, -jnp.inf)
            m_new = jnp.maximum(m[i], jnp.max(s, axis=-1, keepdims=True))
            p = jnp.exp2((s - m_new) * log2e_scale)
            alpha = jnp.exp2((m[i] - m_new) * log2e_scale)
            den[i] = alpha * den[i] + jnp.sum(p, axis=-1, keepdims=True)
            acc[i] = alpha * acc[i] + _dot(p.astype(BF16), vblk)
            m[i] = m_new
            if (i * tq) // tk == j:
                o_ref[i * tq:(i + 1) * tq, :] = (acc[i] / den[i]).astype(BF16)


def _fox(proj, gate, batch, seq):
    n = proj.shape[0]
    blk = COL_QKV_C // HEAD_DIM
    return pl.pallas_call(
        _fox_kernel,
        grid=(batch, N_HEADS),
        in_specs=[
            pl.BlockSpec((seq, HEAD_DIM), lambda b, h: (b, blk + h)),
            pl.BlockSpec((seq, HEAD_DIM), lambda b, h: (b, blk + N_HEADS + h)),
            pl.BlockSpec((seq, HEAD_DIM), lambda b, h: (b, blk + 2 * N_HEADS + h)),
            pl.BlockSpec((seq, SMALL_COLS), lambda b, h: (b, 0)),
        ],
        out_specs=pl.BlockSpec((seq, HEAD_DIM), lambda b, h: (b, h)),
        out_shape=jax.ShapeDtypeStruct((n, WIDTH), BF16),
        scratch_shapes=[
            pltpu.VMEM((seq, 2 * HEAD_DIM), BF16),
            pltpu.VMEM((seq, 2 * HEAD_DIM), BF16),
        ],
        compiler_params=_params(("arbitrary", "arbitrary")),
        name="fox",
    )(proj, proj, proj, gate)


def _merge_kernel(oa_ref, oc_ref, bg_ref, cg_ref, hh_ref, ga_ref, gb_ref, gc_ref, h_ref,
                  convw_ref, wa_ref, wb_ref, wc_ref, wo_ref, out_ref, halo_ref, xs_ref,
                  *, tiles_per_seq):
    tile = h_ref.shape[0]

    @pl.when(pl.program_id(0) % tiles_per_seq == 0)
    def _():
        halo_ref[...] = jnp.zeros_like(halo_ref)

    _conv_stage(cg_ref[...].astype(F32) * hh_ref[...].astype(F32), halo_ref, xs_ref)
    conv = _conv_apply(xs_ref, convw_ref[...])
    sc = (bg_ref[...].astype(F32) * conv).astype(BF16)

    mix = _sigmoid(ga_ref[...].astype(F32)) * _dot(oa_ref[...], wa_ref[...])
    mix = mix + _sigmoid(gb_ref[...].astype(F32)) * _dot(sc, wb_ref[...])
    mix = mix + _sigmoid(gc_ref[...].astype(F32)) * _dot(oc_ref[...], wc_ref[...])
    out_ref[...] = h_ref[...] + _dot(mix.astype(BF16), wo_ref[...])


def _merge(oa, oc, proj, h, conv_sc, w_a, w_b, w_c, w_o, layer, tm, seq):
    n = h.shape[0]
    bch = COL_BCH // WIDTH
    gates = COL_GATES // D_MODEL
    tok = lambda width, blk: pl.BlockSpec((tm, width), lambda i: (i, blk))
    return pl.pallas_call(
        functools.partial(_merge_kernel, tiles_per_seq=seq // tm),
        grid=(n // tm,),
        in_specs=[
            tok(WIDTH, 0), tok(WIDTH, 0),
            tok(WIDTH, bch), tok(WIDTH, bch + 1), tok(WIDTH, bch + 2),
            tok(D_MODEL, gates), tok(D_MODEL, gates + 1), tok(D_MODEL, gates + 2),
            tok(D_MODEL, 0),
            _resident((SC_CONV, WIDTH)),
            _layer_resident((WIDTH, D_MODEL), layer), _layer_resident((WIDTH, D_MODEL), layer),
            _layer_resident((WIDTH, D_MODEL), layer), _layer_resident((D_MODEL, D_MODEL), layer),
        ],
        out_specs=tok(D_MODEL, 0),
        out_shape=jax.ShapeDtypeStruct((n, D_MODEL), F32),
        scratch_shapes=[
            pltpu.VMEM((ROW_HALO, WIDTH), F32),
            pltpu.VMEM((ROW_HALO + tm, WIDTH), F32),
        ],
        compiler_params=_params(("arbitrary",)),
        name="merge",
    )(oa, oc, proj, proj, proj, proj, proj, proj, h, conv_sc, w_a, w_b, w_c, w_o)


def _ffn_kernel(h_ref, g_ref, wup_ref, convw_ref, wdown_ref, gf_ref, out_ref, halo_ref, xs_ref, act_ref,
                *, tiles_per_seq, final_norm):
    tile = h_ref.shape[0]
    tf = FFN_TILE

    @pl.when(pl.program_id(0) % tiles_per_seq == 0)
    def _():
        halo_ref[...] = jnp.zeros_like(halo_ref)

    x = h_ref[...]
    xn = _rms(x, g_ref[...]).astype(BF16)
    def stage(j):
        for part in range(2):
            c0 = part * D_FF + j * tf
            _conv_stage(_dot(xn, wup_ref[:, c0:c0 + tf]), halo_ref.at[2 * j + part], xs_ref.at[2 * j + part])

    def finish(j):
        gate, up = (_conv_apply(xs_ref.at[2 * j + part], convw_ref[:, part * D_FF + j * tf:part * D_FF + (j + 1) * tf])
                    for part in range(2))
        act_ref[:, j * tf:(j + 1) * tf] = (gate * _sigmoid(gate) * up).astype(BF16)

    n_tiles = D_FF // tf
    stage(0)
    for j in range(n_tiles):
        if j + 1 < n_tiles:
            stage(j + 1)
        finish(j)
    y = x + _dot(act_ref[...], wdown_ref[...])
    if final_norm:
        y = _rms(y, gf_ref[...])
    out_ref[...] = y


def _ffn(h, g, w_up, conv_ffn, w_down, g_final, layer, tm, seq, final_norm):
    n = h.shape[0]
    return pl.pallas_call(
        functools.partial(_ffn_kernel, tiles_per_seq=seq // tm, final_norm=final_norm),
        grid=(n // tm,),
        in_specs=[
            pl.BlockSpec((tm, D_MODEL), lambda i: (i, 0)),
            _resident((1, D_MODEL)),
            _layer_resident((D_MODEL, 2 * D_FF), layer),
            _resident((FFN_CONV, 2 * D_FF)),
            _layer_resident((D_FF, D_MODEL), layer),
            _resident((1, D_MODEL)),
        ],
        out_specs=pl.BlockSpec((tm, D_MODEL), lambda i: (i, 0)),
        out_shape=jax.ShapeDtypeStruct((n, D_MODEL), F32),
        scratch_shapes=[
            pltpu.VMEM((2 * (D_FF // FFN_TILE), ROW_HALO, FFN_TILE), F32),
            pltpu.VMEM((2 * (D_FF // FFN_TILE), ROW_HALO + tm, FFN_TILE), F32),
            pltpu.VMEM((tm, D_FF), BF16),
        ],
        compiler_params=_params(("arbitrary",)),
        name="ffn",
    )(h, g, w_up, conv_ffn, w_down, g_final)


def _reorder_kernel(main_ref, next_ref, s1_ref, s2_ref, big_ref, small_ref):
    j = pl.program_id(1)
    cat = jnp.concatenate([main_ref[0], next_ref[0]], axis=1)

    def emit(shift):
        big_ref[0] = cat[:, shift:shift + D_MODEL].astype(BF16)

    pl.when(j < COL_BCH // D_MODEL)(lambda: emit(0))
    pl.when((j >= COL_BCH // D_MODEL) & (j < COL_GATES // D_MODEL))(lambda: emit(2 * N_HEADS))
    pl.when(j >= COL_GATES // D_MODEL)(lambda: emit(3 * N_HEADS))
    lane = lax.broadcasted_iota(jnp.int32, s1_ref.shape[1:], 1)
    small_ref[0] = jnp.where(lane < LANE_F, s1_ref[0], jnp.where(lane < LANE_F + N_HEADS, s2_ref[0], 0.0)).astype(BF16)


def _reorder_w_in(w_in):
    depth, d, width = w_in.shape
    scalars_1 = COL_BCH
    scalars_2 = COL_GATES + 2 * N_HEADS
    assert width == PROJ_COLS + 3 * N_HEADS and COL_BCH % D_MODEL == 0 and COL_GATES % D_MODEL == 0
    assert scalars_1 % SMALL_COLS == LANE_BETA and scalars_2 % SMALL_COLS == LANE_F and LANE_F == 2 * N_HEADS
    per_block = D_MODEL // SMALL_COLS
    return pl.pallas_call(
        _reorder_kernel,
        grid=(depth, PROJ_COLS // D_MODEL),
        in_specs=[
            pl.BlockSpec((1, d, D_MODEL), lambda l, j: (l, 0, j)),
            pl.BlockSpec((1, d, SMALL_COLS), lambda l, j: (l, 0, (j + 1) * per_block)),
            pl.BlockSpec((1, d, SMALL_COLS), lambda l, j: (l, 0, scalars_1 // SMALL_COLS)),
            pl.BlockSpec((1, d, SMALL_COLS), lambda l, j: (l, 0, scalars_2 // SMALL_COLS)),
        ],
        out_specs=[
            pl.BlockSpec((1, d, D_MODEL), lambda l, j: (l, 0, j)),
            pl.BlockSpec((1, d, SMALL_COLS), lambda l, j: (l, 0, 0)),
        ],
        out_shape=[
            jax.ShapeDtypeStruct((depth, d, PROJ_COLS), BF16),
            jax.ShapeDtypeStruct((depth, d, SMALL_COLS), BF16),
        ],
        compiler_params=_params(("arbitrary", "arbitrary")),
        name="reorder_w_in",
    )(w_in, w_in, w_in, w_in)


def _lane_row(pairs):
    row = jnp.zeros((1, SMALL_COLS), F32)
    for lane0, vals in pairs:
        row = row.at[0, lane0:lane0 + N_HEADS].set(vals.astype(F32))
    return row


def kernel(x, norm1_g, w_in, conv_qkv, a_log, dt_bias, gdn_norm, w_br_a, conv_sc, w_br_b, fox_bias, w_br_c, w_o, norm2_g, w_up, conv_ffn, w_down, norm_f):
    batch, seq, d = x.shape
    assert d == D_MODEL and seq % TOKEN_TILE == 0 and seq % GATES_TILE == 0 and seq % FOX_K_TILE == 0
    tm = TOKEN_TILE
    h = x.reshape(batch * seq, d)
    w_big, w_small = _reorder_w_in(w_in)
    w_a, w_b, w_c, w_out = (w.astype(BF16) for w in (w_br_a, w_br_b, w_br_c, w_o))
    w_up16, w_down16 = w_up.astype(BF16), w_down.astype(BF16)
    for l in range(DEPTH):
        proj, small = _inproj(h, norm1_g[l][None, :], w_big, w_small, conv_qkv[l], l, tm, seq)
        bias_row = _lane_row([(LANE_G, dt_bias[l]), (LANE_F, fox_bias[l])])
        alog_row = _lane_row([(LANE_G, a_log[l])])
        gate = _gates(small, bias_row, alog_row, batch, seq)
        oa = _gdn(proj, gate, gdn_norm[l][None, :], batch, seq)
        oc = _fox(proj, gate, batch, seq)
        h = _merge(oa, oc, proj, h, conv_sc[l], w_a, w_b, w_c, w_out, l, tm, seq)
        h = _ffn(h, norm2_g[l][None, :], w_up16, conv_ffn[l], w_down16, norm_f[None, :], l, tm, seq,
                 final_norm=(l == DEPTH - 1))
    return h.reshape(batch, seq, d)
```

```python
import functools

import jax
import jax.numpy as jnp
from jax import lax
from jax.experimental import pallas as pl
from jax.experimental.pallas import tpu as pltpu

F32 = jnp.float32
BF16 = jnp.bfloat16

D_MODEL = 1024
DEPTH = 4
N_HEADS = 4
HEAD_DIM = 128
WIDTH = N_HEADS * HEAD_DIM
GDN_CONV = 4
GDN_CHUNK = 64
SC_CONV = 3
D_FF = 2816
FFN_CONV = 3
EPS = 1e-6

COL_QKV_A = 0
COL_Z_A = 3 * WIDTH
COL_BCH = 4 * WIDTH
COL_QKV_C = 7 * WIDTH
COL_GATES = 10 * WIDTH
PROJ_COLS = 10 * WIDTH + 3 * D_MODEL
LANE_BETA = 0
LANE_G = 4
LANE_F = 8
SMALL_COLS = 128

ROW_HALO = 8
TOKEN_TILE = 512
MERGE_TILE = 1024
GDN_TILE = 512
GATES_TILE = 512
QKV_TILE = 256
FOX_Q_TILE = 256
FOX_K_TILE = 256
FFN_TILE = 256
VMEM_LIMIT = 56 * 1024 * 1024


def _resident(shape):
    nd = len(shape)
    return pl.BlockSpec(shape, lambda *_: (0,) * nd, pipeline_mode=pl.Buffered(1))


def _layer_resident(shape, layer):
    nd = len(shape)
    return pl.BlockSpec((None,) + shape, lambda *_: (layer,) + (0,) * nd, pipeline_mode=pl.Buffered(1))


def _params(sem):
    return pltpu.CompilerParams(dimension_semantics=sem, vmem_limit_bytes=VMEM_LIMIT)


def _rms(x, g):
    return x * lax.rsqrt(jnp.mean(x * x, axis=-1, keepdims=True) + EPS) * g


def _sigmoid(x):
    return 1.0 / (1.0 + jnp.exp(-x))


def _dot(a, b):
    return jnp.dot(a, b, preferred_element_type=F32)


def _dot_nt(a, b):
    return lax.dot_general(a, b, (((1,), (1,)), ((), ())), preferred_element_type=F32)


def _dot_tn(a, b):
    return lax.dot_general(a, b, (((0,), (0,)), ((), ())), preferred_element_type=F32)


def _bdot(a, b):
    return lax.dot_general(a, b, (((2,), (1,)), ((0,), (0,))), preferred_element_type=F32)


def _bdot_nt(a, b):
    return lax.dot_general(a, b, (((2,), (2,)), ((0,), (0,))), preferred_element_type=F32)


def _split3(x):
    h1 = x.astype(BF16)
    r1 = x - h1.astype(F32)
    h2 = r1.astype(BF16)
    h3 = (r1 - h2.astype(F32)).astype(BF16)
    return h1, h2, h3


def _bdot_f32(a, b):
    a_hi = a.astype(BF16).astype(F32)
    b_hi = b.astype(BF16).astype(F32)
    lhs = jnp.concatenate([a_hi, a - a_hi, a_hi], axis=-1).astype(BF16)
    rhs = jnp.concatenate([b_hi, b_hi, b - b_hi], axis=1).astype(BF16)
    return _bdot(lhs, rhs)


def _diff_operands(c):
    shape = c.shape[:-1] + (HEAD_DIM,)
    lane = lax.broadcasted_iota(jnp.int32, shape, len(shape) - 1)
    c1, c2, c3 = (term.astype(F32) for term in _split3(jnp.broadcast_to(c, shape)))
    terms = jnp.where((lane == 0) | (lane == 3), c1, jnp.where((lane == 1) | (lane == 4), c2, c3))
    a = jnp.where(lane < 3, terms, jnp.where(lane < 6, 1.0, 0.0))
    b = jnp.where(lane < 3, 1.0, jnp.where(lane < 6, -terms, 0.0))
    return a.astype(BF16), b.astype(BF16)


def _conv_stage(pre, halo_ref, xs_ref):
    tile = pre.shape[0]
    xs_ref[0:ROW_HALO, :] = halo_ref[...]
    xs_ref[ROW_HALO:ROW_HALO + tile, :] = pre
    halo_ref[...] = pre[tile - ROW_HALO:tile, :]


def _conv_apply(xs_ref, taps):
    ntap = taps.shape[0]
    staged = xs_ref[...]
    out = None
    for k in range(ntap):
        shift = ntap - 1 - k
        rows = staged if shift == 0 else pltpu.roll(staged, shift, axis=0)
        term = rows[ROW_HALO:, :] * taps[k:k + 1, :]
        out = term if out is None else out + term
    return out


def _inproj_kernel(x_ref, g_ref, w_ref, ws_ref, convw_ref, proj_ref, small_ref, halo_ref, xs_ref,
                   *, tiles_per_seq):
    @pl.when(pl.program_id(0) % tiles_per_seq == 0)
    def _():
        halo_ref[...] = jnp.zeros_like(halo_ref)

    xn = _rms(x_ref[...], g_ref[...]).astype(BF16)

    def stage(t):
        c0 = COL_QKV_A + t * QKV_TILE
        _conv_stage(_dot(xn, w_ref[:, c0:c0 + QKV_TILE]), halo_ref.at[t], xs_ref.at[t])

    def epilogue(t):
        c0 = COL_QKV_A + t * QKV_TILE
        part = (t * QKV_TILE) // WIDTH
        y = _conv_apply(xs_ref.at[t], convw_ref[:, c0:c0 + QKV_TILE])
        y = y * _sigmoid(y)
        for lo in range(0, QKV_TILE, HEAD_DIM):
            yh = y[:, lo:lo + HEAD_DIM]
            if part < 2:
                yh = yh * lax.rsqrt(jnp.sum(yh * yh, axis=-1, keepdims=True) + EPS)
            if part == 0:
                yh = yh * (HEAD_DIM ** -0.5)
            proj_ref[:, c0 + lo:c0 + lo + HEAD_DIM] = yh.astype(BF16)

    plain = [(COL_Z_A, COL_BCH)] + [(c0, c0 + D_MODEL) for c0 in range(COL_BCH, PROJ_COLS, D_MODEL)]
    n_qkv = 3 * WIDTH // QKV_TILE
    stage(0)
    for t in range(max(n_qkv, len(plain))):
        if t + 1 < n_qkv:
            stage(t + 1)
        if t < len(plain):
            c0, c1 = plain[t]
            proj_ref[:, c0:c1] = _dot(xn, w_ref[:, c0:c1]).astype(BF16)
        if t < n_qkv:
            epilogue(t)
    small_ref[...] = _dot(xn, ws_ref[...])


def _inproj(h, g, w_big, w_small, conv_qkv, layer, tm, seq):
    n = h.shape[0]
    return pl.pallas_call(
        functools.partial(_inproj_kernel, tiles_per_seq=seq // tm),
        grid=(n // tm,),
        in_specs=[
            pl.BlockSpec((tm, D_MODEL), lambda i: (i, 0)),
            _resident((1, D_MODEL)),
            _layer_resident((D_MODEL, PROJ_COLS), layer),
            _layer_resident((D_MODEL, SMALL_COLS), layer),
            _resident((GDN_CONV, 3 * WIDTH)),
        ],
        out_specs=[
            pl.BlockSpec((tm, PROJ_COLS), lambda i: (i, 0)),
            pl.BlockSpec((tm, SMALL_COLS), lambda i: (i, 0)),
        ],
        out_shape=[
            jax.ShapeDtypeStruct((n, PROJ_COLS), BF16),
            jax.ShapeDtypeStruct((n, SMALL_COLS), F32),
        ],
        scratch_shapes=[
            pltpu.VMEM((3 * WIDTH // QKV_TILE, ROW_HALO, QKV_TILE), F32),
            pltpu.VMEM((3 * WIDTH // QKV_TILE, ROW_HALO + tm, QKV_TILE), F32),
        ],
        compiler_params=_params(("arbitrary",)),
        name="inproj",
    )(h, g, w_big, w_small, conv_qkv)


def _gates_kernel(small_ref, bias_ref, alog_ref, out_ref, carry_ref):
    t = pl.program_id(1)
    tile = small_ref.shape[0]

    @pl.when(t == 0)
    def _():
        carry_ref[...] = jnp.zeros_like(carry_ref)

    x = small_ref[...] + bias_ref[...]
    lane = lax.broadcasted_iota(jnp.int32, x.shape, 1)
    soft = jnp.log1p(jnp.exp(-jnp.abs(x)))
    beta = _sigmoid(x)
    g = -jnp.exp(alog_ref[...]) * (jnp.maximum(x, 0.0) + soft)
    logf = jnp.minimum(x, 0.0) - soft
    vals = jnp.where(lane < LANE_G, beta, jnp.where(lane < LANE_F, g, logf))

    row = lax.broadcasted_iota(jnp.int32, (tile, tile), 0)
    col = lax.broadcasted_iota(jnp.int32, (tile, tile), 1)
    tril = row >= col
    m_full = tril.astype(BF16)
    m_seg = (tril & (row // GDN_CHUNK == col // GDN_CHUNK)).astype(BF16)
    cum = _dot(jnp.concatenate([m_full, m_seg], axis=0), jnp.concatenate(_split3(vals), axis=1))
    cum = cum[:, 0:SMALL_COLS] + (cum[:, SMALL_COLS:2 * SMALL_COLS] + cum[:, 2 * SMALL_COLS:3 * SMALL_COLS])
    cum_full = cum[0:tile] + carry_ref[0:1, :]
    cum_seg = cum[tile:2 * tile]
    carry_ref[0:1, :] = cum_full[tile - 1:tile, :]
    out_ref[...] = jnp.where(lane < LANE_G, beta, jnp.where(lane < LANE_F, cum_seg, cum_full))


def _gates(small, bias_row, alog_row, batch, seq):
    tile = GATES_TILE
    nt = seq // tile
    return pl.pallas_call(
        _gates_kernel,
        grid=(batch, nt),
        in_specs=[
            pl.BlockSpec((tile, SMALL_COLS), lambda b, t: (b * nt + t, 0)),
            _resident((1, SMALL_COLS)),
            _resident((1, SMALL_COLS)),
        ],
        out_specs=pl.BlockSpec((tile, SMALL_COLS), lambda b, t: (b * nt + t, 0)),
        out_shape=jax.ShapeDtypeStruct(small.shape, F32),
        scratch_shapes=[pltpu.VMEM((ROW_HALO, SMALL_COLS), F32)],
        compiler_params=_params(("arbitrary", "arbitrary")),
        name="gates",
    )(small, bias_row, alog_row)


def _unit_lower_inverse(l_strict):
    c = l_strict.shape[-1]
    row = lax.broadcasted_iota(jnp.int32, (c, c), 0)
    col = lax.broadcasted_iota(jnp.int32, (c, c), 1)
    power = -l_strict
    prod = jnp.where(row == col, 1.0, 0.0)[None] + power
    span = 1
    while 2 * span < c:
        power = _bdot_f32(power, power) if span == 1 else stacked[:, :c]
        span *= 2
        if 2 * span < c:
            stacked = _bdot_f32(jnp.concatenate([power, prod], axis=1), power)
            prod = prod + stacked[:, c:]
        else:
            prod = prod + _bdot_f32(prod, power)
    return prod


def _gdn_kernel(qkv_ref, z_ref, gate_ref, norm_ref, o_ref, state_ref):
    tile = qkv_ref.shape[0]

    @pl.when(pl.program_id(1) == 0)
    def _():
        state_ref[...] = jnp.zeros_like(state_ref)

    c = GDN_CHUNK
    nchunk = tile // c
    problems = [(j, h) for j in range(nchunk) for h in range(N_HEADS)]

    def slabs(col0):
        return jnp.stack([qkv_ref[j * c:(j + 1) * c, col0 + h * HEAD_DIM:col0 + (h + 1) * HEAD_DIM]
                          for j, h in problems])

    def lanes(lane0, rows=None):
        return jnp.stack([gate_ref[(j * c if rows is None else j * c + rows):(j + 1) * c,
                                   lane0 + h:lane0 + h + 1] for j, h in problems])

    q = slabs(0).astype(F32)
    k16 = slabs(WIDTH)
    k = k16.astype(F32)
    v = slabs(2 * WIDTH).astype(F32)
    beta = lanes(LANE_BETA)
    gc = lanes(LANE_G)
    g_last = lanes(LANE_G, rows=c - 1)

    row = lax.broadcasted_iota(jnp.int32, (c, c), 0)
    col = lax.broadcasted_iota(jnp.int32, (c, c), 1)
    da, db = _diff_operands(gc)
    decay = jnp.exp(jnp.where((row >= col)[None], _bdot_nt(da, db), -jnp.inf))
    kb = k * beta
    kk_qk = _bdot_nt(jnp.concatenate([kb.astype(BF16), slabs(0)], axis=1), k16)
    l_strict = jnp.where((row > col)[None], kk_qk[:, 0:c] * decay, 0.0)
    attn = (kk_qk[:, c:2 * c] * decay).astype(BF16)
    tm = _unit_lower_inverse(l_strict).astype(BF16)
    uw = _bdot(tm, jnp.concatenate([v * beta, kb * jnp.exp(gc)], axis=-1).astype(BF16))
    u = uw[..., 0:HEAD_DIM]
    w = uw[..., HEAD_DIM:2 * HEAD_DIM].astype(BF16)
    qg = (q * jnp.exp(gc)).astype(BF16)
    kd = (k * jnp.exp(g_last - gc)).astype(BF16)
    s_decay = jnp.exp(g_last)

    u_hi = u.astype(BF16)
    u_lo = (u - u_hi.astype(F32)).astype(BF16)
    wuu = jnp.concatenate([w, u_hi, u_lo], axis=-1)
    kd_t = jnp.stack([_dot_tn(kd[p], wuu[p]) for p in range(len(problems))])
    kd_w = kd_t[..., 0:HEAD_DIM]
    kd_u = kd_t[..., HEAD_DIM:2 * HEAD_DIM] + kd_t[..., 2 * HEAD_DIM:3 * HEAD_DIM]
    kd_w_hi = kd_w.astype(BF16)
    kd_w_split = jnp.concatenate([kd_w_hi, (kd_w - kd_w_hi.astype(F32)).astype(BF16)], axis=-1)

    s = state_ref[...]
    starts = []
    for j in range(nchunk):
        b0, b1 = j * N_HEADS, (j + 1) * N_HEADS
        s16 = s.astype(BF16)
        starts.append(s16)
        s = s * s_decay[b0:b1] + kd_u[b0:b1] - _bdot(kd_w_split[b0:b1], jnp.concatenate([s16, s16], axis=1))
    state_ref[...] = s

    s_start = jnp.concatenate(starts, axis=0)
    wq_s = _bdot(jnp.concatenate([w, qg], axis=1), s_start)
    v_new = (u - wq_s[:, 0:c]).astype(BF16)
    o = _rms(wq_s[:, c:2 * c] + _bdot(attn, v_new), norm_ref[...][None])
    for p, (j, h) in enumerate(problems):
        lo = h * HEAD_DIM
        z = z_ref[j * c:(j + 1) * c, lo:lo + HEAD_DIM].astype(F32)
        o_ref[j * c:(j + 1) * c, lo:lo + HEAD_DIM] = (o[p] * (z * _sigmoid(z))).astype(BF16)


def _gdn(proj, gate, gdn_norm, batch, seq):
    tile = GDN_TILE
    nt = seq // tile
    n = proj.shape[0]
    return pl.pallas_call(
        _gdn_kernel,
        grid=(batch, nt),
        in_specs=[
            pl.BlockSpec((tile, 3 * WIDTH), lambda b, t: (b * nt + t, COL_QKV_A // (3 * WIDTH))),
            pl.BlockSpec((tile, WIDTH), lambda b, t: (b * nt + t, COL_Z_A // WIDTH)),
            pl.BlockSpec((tile, SMALL_COLS), lambda b, t: (b * nt + t, 0)),
            _resident((1, HEAD_DIM)),
        ],
        out_specs=pl.BlockSpec((tile, WIDTH), lambda b, t: (b * nt + t, 0)),
        out_shape=jax.ShapeDtypeStruct((n, WIDTH), BF16),
        scratch_shapes=[pltpu.VMEM((N_HEADS, HEAD_DIM, HEAD_DIM), F32)],
        compiler_params=_params(("arbitrary", "arbitrary")),
        name="gdn",
    )(proj, proj, gate, gdn_norm)


def _fox_kernel(q_ref, k_ref, v_ref, gate_ref, o_ref, qa_ref, ka_ref):
    seq = q_ref.shape[0]
    tq, tk = FOX_Q_TILE, FOX_K_TILE
    scale = HEAD_DIM ** -0.5
    lane = lax.broadcasted_iota(jnp.int32, (seq, SMALL_COLS), 1)
    c = jnp.sum(jnp.where(lane == LANE_F + pl.program_id(1), gate_ref[...], 0.0), axis=-1, keepdims=True)
    ca, cb = _diff_operands(c * (1.0 / scale))
    qa_ref[:, 0:HEAD_DIM] = q_ref[...]
    qa_ref[:, HEAD_DIM:2 * HEAD_DIM] = ca
    ka_ref[:, 0:HEAD_DIM] = k_ref[...]
    ka_ref[:, HEAD_DIM:2 * HEAD_DIM] = cb
    row = lax.broadcasted_iota(jnp.int32, (tq, tk), 0)
    col = lax.broadcasted_iota(jnp.int32, (tq, tk), 1)
    log2e_scale = scale * 1.4426950408889634
    n_q = seq // tq
    m = [jnp.full((tq, 1), -jnp.inf, F32) for _ in range(n_q)]
    den = [jnp.zeros((tq, 1), F32) for _ in range(n_q)]
    acc = [jnp.zeros((tq, HEAD_DIM), F32) for _ in range(n_q)]
    for j in range(seq // tk):
        kblk = ka_ref[j * tk:(j + 1) * tk, :]
        vblk = v_ref[j * tk:(j + 1) * tk, :]
        for i in range((j * tk) // tq, n_q):
            s = _dot_nt(qa_ref[i * tq:(i + 1) * tq, :], kblk)
            if (i * tq) // tk == j:
                s = jnp.where(row + (i * tq - j * tk) >= col, s, -jnp.inf)
            m_new = jnp.maximum(m[i], jnp.max(s, axis=-1, keepdims=True))
            p = jnp.exp2((s - m_new) * log2e_scale)
            alpha = jnp.exp2((m[i] - m_new) * log2e_scale)
            den[i] = alpha * den[i] + jnp.sum(p, axis=-1, keepdims=True)
            acc[i] = alpha * acc[i] + _dot(p.astype(BF16), vblk)
            m[i] = m_new
            if (i * tq) // tk == j:
                o_ref[i * tq:(i + 1) * tq, :] = (acc[i] / den[i]).astype(BF16)


def _fox(proj, gate, batch, seq):
    n = proj.shape[0]
    blk = COL_QKV_C // HEAD_DIM
    return pl.pallas_call(
        _fox_kernel,
        grid=(batch, N_HEADS),
        in_specs=[
            pl.BlockSpec((seq, HEAD_DIM), lambda b, h: (b, blk + h)),
            pl.BlockSpec((seq, HEAD_DIM), lambda b, h: (b, blk + N_HEADS + h)),
            pl.BlockSpec((seq, HEAD_DIM), lambda b, h: (b, blk + 2 * N_HEADS + h)),
            pl.BlockSpec((seq, SMALL_COLS), lambda b, h: (b, 0)),
        ],
        out_specs=pl.BlockSpec((seq, HEAD_DIM), lambda b, h: (b, h)),
        out_shape=jax.ShapeDtypeStruct((n, WIDTH), BF16),
        scratch_shapes=[
            pltpu.VMEM((seq, 2 * HEAD_DIM), BF16),
            pltpu.VMEM((seq, 2 * HEAD_DIM), BF16),
        ],
        compiler_params=_params(("arbitrary", "arbitrary")),
        name="fox",
    )(proj, proj, proj, gate)


def _merge_kernel(oa_ref, oc_ref, bg_ref, cg_ref, hh_ref, ga_ref, gb_ref, gc_ref, h_ref,
                  convw_ref, wa_ref, wb_ref, wc_ref, wo_ref, out_ref, halo_ref, xs_ref,
                  *, tiles_per_seq):
    tile = h_ref.shape[0]

    @pl.when(pl.program_id(0) % tiles_per_seq == 0)
    def _():
        halo_ref[...] = jnp.zeros_like(halo_ref)

    _conv_stage(cg_ref[...].astype(F32) * hh_ref[...].astype(F32), halo_ref, xs_ref)
    conv = _conv_apply(xs_ref, convw_ref[...])
    sc = (bg_ref[...].astype(F32) * conv).astype(BF16)

    mix = _sigmoid(ga_ref[...].astype(F32)) * _dot(oa_ref[...], wa_ref[...])
    mix = mix + _sigmoid(gb_ref[...].astype(F32)) * _dot(sc, wb_ref[...])
    mix = mix + _sigmoid(gc_ref[...].astype(F32)) * _dot(oc_ref[...], wc_ref[...])
    out_ref[...] = h_ref[...] + _dot(mix.astype(BF16), wo_ref[...])


def _merge(oa, oc, proj, h, conv_sc, w_a, w_b, w_c, w_o, layer, tm, seq):
    n = h.shape[0]
    bch = COL_BCH // WIDTH
    gates = COL_GATES // D_MODEL
    tok = lambda width, blk: pl.BlockSpec((tm, width), lambda i: (i, blk))
    return pl.pallas_call(
        functools.partial(_merge_kernel, tiles_per_seq=seq // tm),
        grid=(n // tm,),
        in_specs=[
            tok(WIDTH, 0), tok(WIDTH, 0),
            tok(WIDTH, bch), tok(WIDTH, bch + 1), tok(WIDTH, bch + 2),
            tok(D_MODEL, gates), tok(D_MODEL, gates + 1), tok(D_MODEL, gates + 2),
            tok(D_MODEL, 0),
            _resident((SC_CONV, WIDTH)),
            _layer_resident((WIDTH, D_MODEL), layer), _layer_resident((WIDTH, D_MODEL), layer),
            _layer_resident((WIDTH, D_MODEL), layer), _layer_resident((D_MODEL, D_MODEL), layer),
        ],
        out_specs=tok(D_MODEL, 0),
        out_shape=jax.ShapeDtypeStruct((n, D_MODEL), F32),
        scratch_shapes=[
            pltpu.VMEM((ROW_HALO, WIDTH), F32),
            pltpu.VMEM((ROW_HALO + tm, WIDTH), F32),
        ],
        compiler_params=_params(("arbitrary",)),
        name="merge",
    )(oa, oc, proj, proj, proj, proj, proj, proj, h, conv_sc, w_a, w_b, w_c, w_o)


def _ffn_kernel(h_ref, g_ref, wup_ref, convw_ref, wdown_ref, gf_ref, out_ref, halo_ref, xs_ref, act_ref,
                *, tiles_per_seq, final_norm):
    tile = h_ref.shape[0]
    tf = FFN_TILE

    @pl.when(pl.program_id(0) % tiles_per_seq == 0)
    def _():
        halo_ref[...] = jnp.zeros_like(halo_ref)

    x = h_ref[...]
    xn = _rms(x, g_ref[...]).astype(BF16)
    def stage(j):
        for part in range(2):
            c0 = part * D_FF + j * tf
            _conv_stage(_dot(xn, wup_ref[:, c0:c0 + tf]), halo_ref.at[2 * j + part], xs_ref.at[2 * j + part])

    def finish(j):
        gate, up = (_conv_apply(xs_ref.at[2 * j + part], convw_ref[:, part * D_FF + j * tf:part * D_FF + (j + 1) * tf])
                    for part in range(2))
        act_ref[:, j * tf:(j + 1) * tf] = (gate * _sigmoid(gate) * up).astype(BF16)

    n_tiles = D_FF // tf
    stage(0)
    for j in range(n_tiles):
        if j + 1 < n_tiles:
            stage(j + 1)
        finish(j)
    y = x + _dot(act_ref[...], wdown_ref[...])
    if final_norm:
        y = _rms(y, gf_ref[...])
    out_ref[...] = y


def _ffn(h, g, w_up, conv_ffn, w_down, g_final, layer, tm, seq, final_norm):
    n = h.shape[0]
    return pl.pallas_call(
        functools.partial(_ffn_kernel, tiles_per_seq=seq // tm, final_norm=final_norm),
        grid=(n // tm,),
        in_specs=[
            pl.BlockSpec((tm, D_MODEL), lambda i: (i, 0)),
            _resident((1, D_MODEL)),
            _layer_resident((D_MODEL, 2 * D_FF), layer),
            _resident((FFN_CONV, 2 * D_FF)),
            _layer_resident((D_FF, D_MODEL), layer),
            _resident((1, D_MODEL)),
        ],
        out_specs=pl.BlockSpec((tm, D_MODEL), lambda i: (i, 0)),
        out_shape=jax.ShapeDtypeStruct((n, D_MODEL), F32),
        scratch_shapes=[
            pltpu.VMEM((2 * (D_FF // FFN_TILE), ROW_HALO, FFN_TILE), F32),
            pltpu.VMEM((2 * (D_FF // FFN_TILE), ROW_HALO + tm, FFN_TILE), F32),
            pltpu.VMEM((tm, D_FF), BF16),
        ],
        compiler_params=_params(("arbitrary",)),
        name="ffn",
    )(h, g, w_up, conv_ffn, w_down, g_final)


def _reorder_kernel(main_ref, next_ref, s1_ref, s2_ref, big_ref, small_ref):
    j = pl.program_id(1)
    cat = jnp.concatenate([main_ref[0], next_ref[0]], axis=1)

    def emit(shift):
        big_ref[0] = cat[:, shift:shift + D_MODEL].astype(BF16)

    pl.when(j < COL_BCH // D_MODEL)(lambda: emit(0))
    pl.when((j >= COL_BCH // D_MODEL) & (j < COL_GATES // D_MODEL))(lambda: emit(2 * N_HEADS))
    pl.when(j >= COL_GATES // D_MODEL)(lambda: emit(3 * N_HEADS))
    lane = lax.broadcasted_iota(jnp.int32, s1_ref.shape[1:], 1)
    small_ref[0] = jnp.where(lane < LANE_F, s1_ref[0], jnp.where(lane < LANE_F + N_HEADS, s2_ref[0], 0.0)).astype(BF16)


def _reorder_w_in(w_in):
    depth, d, width = w_in.shape
    scalars_1 = COL_BCH
    scalars_2 = COL_GATES + 2 * N_HEADS
    assert width == PROJ_COLS + 3 * N_HEADS and COL_BCH % D_MODEL == 0 and COL_GATES % D_MODEL == 0
    assert scalars_1 % SMALL_COLS == LANE_BETA and scalars_2 % SMALL_COLS == LANE_F and LANE_F == 2 * N_HEADS
    per_block = D_MODEL // SMALL_COLS
    return pl.pallas_call(
        _reorder_kernel,
        grid=(depth, PROJ_COLS // D_MODEL),
        in_specs=[
            pl.BlockSpec((1, d, D_MODEL), lambda l, j: (l, 0, j)),
            pl.BlockSpec((1, d, SMALL_COLS), lambda l, j: (l, 0, (j + 1) * per_block)),
            pl.BlockSpec((1, d, SMALL_COLS), lambda l, j: (l, 0, scalars_1 // SMALL_COLS)),
            pl.BlockSpec((1, d, SMALL_COLS), lambda l, j: (l, 0, scalars_2 // SMALL_COLS)),
        ],
        out_specs=[
            pl.BlockSpec((1, d, D_MODEL), lambda l, j: (l, 0, j)),
            pl.BlockSpec((1, d, SMALL_COLS), lambda l, j: (l, 0, 0)),
        ],
        out_shape=[
            jax.ShapeDtypeStruct((depth, d, PROJ_COLS), BF16),
            jax.ShapeDtypeStruct((depth, d, SMALL_COLS), BF16),
        ],
        compiler_params=_params(("arbitrary", "arbitrary")),
        name="reorder_w_in",
    )(w_in, w_in, w_in, w_in)


def _lane_row(pairs):
    row = jnp.zeros((1, SMALL_COLS), F32)
    for lane0, vals in pairs:
        row = row.at[0, lane0:lane0 + N_HEADS].set(vals.astype(F32))
    return row


def kernel(x, norm1_g, w_in, conv_qkv, a_log, dt_bias, gdn_norm, w_br_a, conv_sc, w_br_b, fox_bias, w_br_c, w_o, norm2_g, w_up, conv_ffn, w_down, norm_f):
    batch, seq, d = x.shape
    assert d == D_MODEL and seq % TOKEN_TILE == 0 and seq % GATES_TILE == 0 and seq % FOX_K_TILE == 0 and seq % MERGE_TILE == 0 and seq % GDN_TILE == 0
    tm = TOKEN_TILE
    h = x.reshape(batch * seq, d)
    w_big, w_small = _reorder_w_in(w_in)
    w_a, w_b, w_c, w_out = (w.astype(BF16) for w in (w_br_a, w_br_b, w_br_c, w_o))
    w_up16, w_down16 = w_up.astype(BF16), w_down.astype(BF16)
    for l in range(DEPTH):
        proj, small = _inproj(h, norm1_g[l][None, :], w_big, w_small, conv_qkv[l], l, tm, seq)
        bias_row = _lane_row([(LANE_G, dt_bias[l]), (LANE_F, fox_bias[l])])
        alog_row = _lane_row([(LANE_G, a_log[l])])
        gate = _gates(small, bias_row, alog_row, batch, seq)
        oa = _gdn(proj, gate, gdn_norm[l][None, :], batch, seq)
        oc = _fox(proj, gate, batch, seq)
        h = _merge(oa, oc, proj, h, conv_sc[l], w_a, w_b, w_c, w_out, l, MERGE_TILE, seq)
        h = _ffn(h, norm2_g[l][None, :], w_up16, conv_ffn[l], w_down16, norm_f[None, :], l, tm, seq,
                 final_norm=(l == DEPTH - 1))
    return h.reshape(batch, seq, d)
```

```python
import functools

import jax
import jax.numpy as jnp
from jax import lax
from jax.experimental import pallas as pl
from jax.experimental.pallas import tpu as pltpu

F32 = jnp.float32
BF16 = jnp.bfloat16

D_MODEL = 1024
DEPTH = 4
N_HEADS = 4
HEAD_DIM = 128
WIDTH = N_HEADS * HEAD_DIM
GDN_CONV = 4
GDN_CHUNK = 64
SC_CONV = 3
D_FF = 2816
FFN_CONV = 3
EPS = 1e-6

COL_QKV_A = 0
COL_Z_A = 3 * WIDTH
COL_BCH = 4 * WIDTH
COL_QKV_C = 7 * WIDTH
COL_GATES = 10 * WIDTH
PROJ_COLS = 10 * WIDTH + 3 * D_MODEL
LANE_BETA = 0
LANE_G = 4
LANE_F = 8
SMALL_COLS = 128

ROW_HALO = 8
TOKEN_TILE = 512
MERGE_TILE = 1024
GDN_TILE = 512
GATES_TILE = 512
QKV_TILE = 256
FOX_Q_TILE = 256
FOX_K_TILE = 256
FFN_TILE = 256
VMEM_LIMIT = 56 * 1024 * 1024


def _resident(shape):
    nd = len(shape)
    return pl.BlockSpec(shape, lambda *_: (0,) * nd, pipeline_mode=pl.Buffered(1))


def _layer_resident(shape, layer):
    nd = len(shape)
    return pl.BlockSpec((None,) + shape, lambda *_: (layer,) + (0,) * nd, pipeline_mode=pl.Buffered(1))


def _params(sem):
    return pltpu.CompilerParams(dimension_semantics=sem, vmem_limit_bytes=VMEM_LIMIT)


def _rms(x, g):
    return x * lax.rsqrt(jnp.mean(x * x, axis=-1, keepdims=True) + EPS) * g


def _sigmoid(x):
    return 1.0 / (1.0 + jnp.exp(-x))


def _dot(a, b):
    return jnp.dot(a, b, preferred_element_type=F32)


def _dot_nt(a, b):
    return lax.dot_general(a, b, (((1,), (1,)), ((), ())), preferred_element_type=F32)


def _dot_tn(a, b):
    return lax.dot_general(a, b, (((0,), (0,)), ((), ())), preferred_element_type=F32)


def _bdot(a, b):
    return lax.dot_general(a, b, (((2,), (1,)), ((0,), (0,))), preferred_element_type=F32)


def _bdot_nt(a, b):
    return lax.dot_general(a, b, (((2,), (2,)), ((0,), (0,))), preferred_element_type=F32)


def _split3(x):
    h1 = x.astype(BF16)
    r1 = x - h1.astype(F32)
    h2 = r1.astype(BF16)
    h3 = (r1 - h2.astype(F32)).astype(BF16)
    return h1, h2, h3


def _bdot_f32(a, b):
    a_hi = a.astype(BF16).astype(F32)
    b_hi = b.astype(BF16).astype(F32)
    lhs = jnp.concatenate([a_hi, a - a_hi, a_hi], axis=-1).astype(BF16)
    rhs = jnp.concatenate([b_hi, b_hi, b - b_hi], axis=1).astype(BF16)
    return _bdot(lhs, rhs)


def _diff_operands(c):
    shape = c.shape[:-1] + (HEAD_DIM,)
    lane = lax.broadcasted_iota(jnp.int32, shape, len(shape) - 1)
    c1, c2, c3 = (term.astype(F32) for term in _split3(jnp.broadcast_to(c, shape)))
    terms = jnp.where((lane == 0) | (lane == 3), c1, jnp.where((lane == 1) | (lane == 4), c2, c3))
    a = jnp.where(lane < 3, terms, jnp.where(lane < 6, 1.0, 0.0))
    b = jnp.where(lane < 3, 1.0, jnp.where(lane < 6, -terms, 0.0))
    return a.astype(BF16), b.astype(BF16)


def _conv_stage(pre, halo_ref, xs_ref):
    tile = pre.shape[0]
    xs_ref[0:ROW_HALO, :] = halo_ref[...]
    xs_ref[ROW_HALO:ROW_HALO + tile, :] = pre
    halo_ref[...] = pre[tile - ROW_HALO:tile, :]


def _conv_apply(xs_ref, taps):
    ntap = taps.shape[0]
    staged = xs_ref[...]
    prev = pltpu.roll(staged, 1, axis=0)
    out = None
    for pair in range((ntap + 1) // 2):
        k = ntap - 1 - 2 * pair
        term = staged * taps[k:k + 1, :]
        if k > 0:
            term = term + prev * taps[k - 1:k, :]
        if pair > 0:
            term = pltpu.roll(term, 2 * pair, axis=0)
        out = term if out is None else out + term
    return out[ROW_HALO:, :]


def _inproj_kernel(x_ref, g_ref, w_ref, ws_ref, convw_ref, proj_ref, small_ref, halo_ref, xs_ref,
                   *, tiles_per_seq):
    @pl.when(pl.program_id(0) % tiles_per_seq == 0)
    def _():
        halo_ref[...] = jnp.zeros_like(halo_ref)

    xn = _rms(x_ref[...], g_ref[...]).astype(BF16)

    def stage(t):
        c0 = COL_QKV_A + t * QKV_TILE
        _conv_stage(_dot(xn, w_ref[:, c0:c0 + QKV_TILE]), halo_ref.at[t], xs_ref.at[t])

    def epilogue(t):
        c0 = COL_QKV_A + t * QKV_TILE
        part = (t * QKV_TILE) // WIDTH
        y = _conv_apply(xs_ref.at[t], convw_ref[:, c0:c0 + QKV_TILE])
        y = y * _sigmoid(y)
        for lo in range(0, QKV_TILE, HEAD_DIM):
            yh = y[:, lo:lo + HEAD_DIM]
            if part < 2:
                yh = yh * lax.rsqrt(jnp.sum(yh * yh, axis=-1, keepdims=True) + EPS)
            if part == 0:
                yh = yh * (HEAD_DIM ** -0.5)
            proj_ref[:, c0 + lo:c0 + lo + HEAD_DIM] = yh.astype(BF16)

    plain = [(COL_Z_A, COL_BCH)] + [(c0, c0 + D_MODEL) for c0 in range(COL_BCH, PROJ_COLS, D_MODEL)]
    n_qkv = 3 * WIDTH // QKV_TILE
    stage(0)
    for t in range(max(n_qkv, len(plain))):
        if t + 1 < n_qkv:
            stage(t + 1)
        if t < len(plain):
            c0, c1 = plain[t]
            proj_ref[:, c0:c1] = _dot(xn, w_ref[:, c0:c1]).astype(BF16)
        if t < n_qkv:
            epilogue(t)
    small_ref[...] = _dot(xn, ws_ref[...])


def _inproj(h, g, w_big, w_small, conv_qkv, layer, tm, seq):
    n = h.shape[0]
    return pl.pallas_call(
        functools.partial(_inproj_kernel, tiles_per_seq=seq // tm),
        grid=(n // tm,),
        in_specs=[
            pl.BlockSpec((tm, D_MODEL), lambda i: (i, 0)),
            _resident((1, D_MODEL)),
            _layer_resident((D_MODEL, PROJ_COLS), layer),
            _layer_resident((D_MODEL, SMALL_COLS), layer),
            _resident((GDN_CONV, 3 * WIDTH)),
        ],
        out_specs=[
            pl.BlockSpec((tm, PROJ_COLS), lambda i: (i, 0)),
            pl.BlockSpec((tm, SMALL_COLS), lambda i: (i, 0)),
        ],
        out_shape=[
            jax.ShapeDtypeStruct((n, PROJ_COLS), BF16),
            jax.ShapeDtypeStruct((n, SMALL_COLS), F32),
        ],
        scratch_shapes=[
            pltpu.VMEM((3 * WIDTH // QKV_TILE, ROW_HALO, QKV_TILE), F32),
            pltpu.VMEM((3 * WIDTH // QKV_TILE, ROW_HALO + tm, QKV_TILE), F32),
        ],
        compiler_params=_params(("arbitrary",)),
        name="inproj",
    )(h, g, w_big, w_small, conv_qkv)


def _gates_kernel(small_ref, bias_ref, alog_ref, out_ref, carry_ref):
    t = pl.program_id(1)
    tile = small_ref.shape[0]

    @pl.when(t == 0)
    def _():
        carry_ref[...] = jnp.zeros_like(carry_ref)

    x = small_ref[...] + bias_ref[...]
    lane = lax.broadcasted_iota(jnp.int32, x.shape, 1)
    soft = jnp.log1p(jnp.exp(-jnp.abs(x)))
    beta = _sigmoid(x)
    g = -jnp.exp(alog_ref[...]) * (jnp.maximum(x, 0.0) + soft)
    logf = jnp.minimum(x, 0.0) - soft
    vals = jnp.where(lane < LANE_G, beta, jnp.where(lane < LANE_F, g, logf))

    row = lax.broadcasted_iota(jnp.int32, (tile, tile), 0)
    col = lax.broadcasted_iota(jnp.int32, (tile, tile), 1)
    tril = row >= col
    m_full = tril.astype(BF16)
    m_seg = (tril & (row // GDN_CHUNK == col // GDN_CHUNK)).astype(BF16)
    cum = _dot(jnp.concatenate([m_full, m_seg], axis=0), jnp.concatenate(_split3(vals), axis=1))
    cum = cum[:, 0:SMALL_COLS] + (cum[:, SMALL_COLS:2 * SMALL_COLS] + cum[:, 2 * SMALL_COLS:3 * SMALL_COLS])
    cum_full = cum[0:tile] + carry_ref[0:1, :]
    cum_seg = cum[tile:2 * tile]
    carry_ref[0:1, :] = cum_full[tile - 1:tile, :]
    out_ref[...] = jnp.where(lane < LANE_G, beta, jnp.where(lane < LANE_F, cum_seg, cum_full))


def _gates(small, bias_row, alog_row, batch, seq):
    tile = GATES_TILE
    nt = seq // tile
    return pl.pallas_call(
        _gates_kernel,
        grid=(batch, nt),
        in_specs=[
            pl.BlockSpec((tile, SMALL_COLS), lambda b, t: (b * nt + t, 0)),
            _resident((1, SMALL_COLS)),
            _resident((1, SMALL_COLS)),
        ],
        out_specs=pl.BlockSpec((tile, SMALL_COLS), lambda b, t: (b * nt + t, 0)),
        out_shape=jax.ShapeDtypeStruct(small.shape, F32),
        scratch_shapes=[pltpu.VMEM((ROW_HALO, SMALL_COLS), F32)],
        compiler_params=_params(("arbitrary", "arbitrary")),
        name="gates",
    )(small, bias_row, alog_row)


def _unit_lower_inverse(l_strict):
    c = l_strict.shape[-1]
    row = lax.broadcasted_iota(jnp.int32, (c, c), 0)
    col = lax.broadcasted_iota(jnp.int32, (c, c), 1)
    power = -l_strict
    prod = jnp.where(row == col, 1.0, 0.0)[None] + power
    span = 1
    while 2 * span < c:
        power = _bdot_f32(power, power) if span == 1 else stacked[:, :c]
        span *= 2
        if 2 * span < c:
            stacked = _bdot_f32(jnp.concatenate([power, prod], axis=1), power)
            prod = prod + stacked[:, c:]
        else:
            prod = prod + _bdot_f32(prod, power)
    return prod


def _gdn_kernel(qkv_ref, z_ref, gate_ref, norm_ref, o_ref, state_ref):
    tile = qkv_ref.shape[0]

    @pl.when(pl.program_id(1) == 0)
    def _():
        state_ref[...] = jnp.zeros_like(state_ref)

    c = GDN_CHUNK
    nchunk = tile // c
    problems = [(j, h) for j in range(nchunk) for h in range(N_HEADS)]

    def slabs(col0):
        return jnp.stack([qkv_ref[j * c:(j + 1) * c, col0 + h * HEAD_DIM:col0 + (h + 1) * HEAD_DIM]
                          for j, h in problems])

    def lanes(lane0, rows=None):
        return jnp.stack([gate_ref[(j * c if rows is None else j * c + rows):(j + 1) * c,
                                   lane0 + h:lane0 + h + 1] for j, h in problems])

    q = slabs(0).astype(F32)
    k16 = slabs(WIDTH)
    k = k16.astype(F32)
    v = slabs(2 * WIDTH).astype(F32)
    beta = lanes(LANE_BETA)
    gc = lanes(LANE_G)
    g_last = lanes(LANE_G, rows=c - 1)

    row = lax.broadcasted_iota(jnp.int32, (c, c), 0)
    col = lax.broadcasted_iota(jnp.int32, (c, c), 1)
    da, db = _diff_operands(gc)
    decay = jnp.exp(jnp.where((row >= col)[None], _bdot_nt(da, db), -jnp.inf))
    kb = k * beta
    kk_qk = _bdot_nt(jnp.concatenate([kb.astype(BF16), slabs(0)], axis=1), k16)
    l_strict = jnp.where((row > col)[None], kk_qk[:, 0:c] * decay, 0.0)
    attn = (kk_qk[:, c:2 * c] * decay).astype(BF16)
    tm = _unit_lower_inverse(l_strict).astype(BF16)
    uw = _bdot(tm, jnp.concatenate([v * beta, kb * jnp.exp(gc)], axis=-1).astype(BF16))
    u = uw[..., 0:HEAD_DIM]
    w = uw[..., HEAD_DIM:2 * HEAD_DIM].astype(BF16)
    qg = (q * jnp.exp(gc)).astype(BF16)
    kd = (k * jnp.exp(g_last - gc)).astype(BF16)
    s_decay = jnp.exp(g_last)

    u_hi = u.astype(BF16)
    u_lo = (u - u_hi.astype(F32)).astype(BF16)
    wuu = jnp.concatenate([w, u_hi, u_lo], axis=-1)
    kd_t = jnp.stack([_dot_tn(kd[p], wuu[p]) for p in range(len(problems))])
    kd_w = kd_t[..., 0:HEAD_DIM]
    kd_u = kd_t[..., HEAD_DIM:2 * HEAD_DIM] + kd_t[..., 2 * HEAD_DIM:3 * HEAD_DIM]
    kd_w_hi = kd_w.astype(BF16)
    kd_w_split = jnp.concatenate([kd_w_hi, (kd_w - kd_w_hi.astype(F32)).astype(BF16)], axis=-1)

    s = state_ref[...]
    starts = []
    for j in range(nchunk):
        b0, b1 = j * N_HEADS, (j + 1) * N_HEADS
        s16 = s.astype(BF16)
        starts.append(s16)
        s = s * s_decay[b0:b1] + kd_u[b0:b1] - _bdot(kd_w_split[b0:b1], jnp.concatenate([s16, s16], axis=1))
    state_ref[...] = s

    s_start = jnp.concatenate(starts, axis=0)
    wq_s = _bdot(jnp.concatenate([w, qg], axis=1), s_start)
    v_new = (u - wq_s[:, 0:c]).astype(BF16)
    o = _rms(wq_s[:, c:2 * c] + _bdot(attn, v_new), norm_ref[...][None])
    for p, (j, h) in enumerate(problems):
        lo = h * HEAD_DIM
        z = z_ref[j * c:(j + 1) * c, lo:lo + HEAD_DIM].astype(F32)
        o_ref[j * c:(j + 1) * c, lo:lo + HEAD_DIM] = (o[p] * (z * _sigmoid(z))).astype(BF16)


def _gdn(proj, gate, gdn_norm, batch, seq):
    tile = GDN_TILE
    nt = seq // tile
    n = proj.shape[0]
    return pl.pallas_call(
        _gdn_kernel,
        grid=(batch, nt),
        in_specs=[
            pl.BlockSpec((tile, 3 * WIDTH), lambda b, t: (b * nt + t, COL_QKV_A // (3 * WIDTH))),
            pl.BlockSpec((tile, WIDTH), lambda b, t: (b * nt + t, COL_Z_A // WIDTH)),
            pl.BlockSpec((tile, SMALL_COLS), lambda b, t: (b * nt + t, 0)),
            _resident((1, HEAD_DIM)),
        ],
        out_specs=pl.BlockSpec((tile, WIDTH), lambda b, t: (b * nt + t, 0)),
        out_shape=jax.ShapeDtypeStruct((n, WIDTH), BF16),
        scratch_shapes=[pltpu.VMEM((N_HEADS, HEAD_DIM, HEAD_DIM), F32)],
        compiler_params=_params(("arbitrary", "arbitrary")),
        name="gdn",
    )(proj, proj, gate, gdn_norm)


def _fox_kernel(q_ref, k_ref, v_ref, gate_ref, o_ref, qa_ref, ka_ref):
    seq = q_ref.shape[0]
    tq, tk = FOX_Q_TILE, FOX_K_TILE
    scale = HEAD_DIM ** -0.5
    lane = lax.broadcasted_iota(jnp.int32, (seq, SMALL_COLS), 1)
    c = jnp.sum(jnp.where(lane == LANE_F + pl.program_id(1), gate_ref[...], 0.0), axis=-1, keepdims=True)
    ca, cb = _diff_operands(c * (1.0 / scale))
    qa_ref[:, 0:HEAD_DIM] = q_ref[...]
    qa_ref[:, HEAD_DIM:2 * HEAD_DIM] = ca
    ka_ref[:, 0:HEAD_DIM] = k_ref[...]
    ka_ref[:, HEAD_DIM:2 * HEAD_DIM] = cb
    row = lax.broadcasted_iota(jnp.int32, (tq, tk), 0)
    col = lax.broadcasted_iota(jnp.int32, (tq, tk), 1)
    log2e_scale = scale * 1.4426950408889634
    n_q = seq // tq
    m = [jnp.full((tq, 1), -jnp.inf, F32) for _ in range(n_q)]
    den = [jnp.zeros((tq, 1), F32) for _ in range(n_q)]
    acc = [jnp.zeros((tq, HEAD_DIM), F32) for _ in range(n_q)]
    for j in range(seq // tk):
        kblk = ka_ref[j * tk:(j + 1) * tk, :]
        vblk = v_ref[j * tk:(j + 1) * tk, :]
        for i in range((j * tk) // tq, n_q):
            s = _dot_nt(qa_ref[i * tq:(i + 1) * tq, :], kblk)
            if (i * tq) // tk == j:
                s = jnp.where(row + (i * tq - j * tk) >= col, s, -jnp.inf)
            m_new = jnp.maximum(m[i], jnp.max(s, axis=-1, keepdims=True))
            p = jnp.exp2((s - m_new) * log2e_scale)
            alpha = jnp.exp2((m[i] - m_new) * log2e_scale)
            den[i] = alpha * den[i] + jnp.sum(p, axis=-1, keepdims=True)
            acc[i] = alpha * acc[i] + _dot(p.astype(BF16), vblk)
            m[i] = m_new
            if (i * tq) // tk == j:
                o_ref[i * tq:(i + 1) * tq, :] = (acc[i] / den[i]).astype(BF16)


def _fox(proj, gate, batch, seq):
    n = proj.shape[0]
    blk = COL_QKV_C // HEAD_DIM
    return pl.pallas_call(
        _fox_kernel,
        grid=(batch, N_HEADS),
        in_specs=[
            pl.BlockSpec((seq, HEAD_DIM), lambda b, h: (b, blk + h)),
            pl.BlockSpec((seq, HEAD_DIM), lambda b, h: (b, blk + N_HEADS + h)),
            pl.BlockSpec((seq, HEAD_DIM), lambda b, h: (b, blk + 2 * N_HEADS + h)),
            pl.BlockSpec((seq, SMALL_COLS), lambda b, h: (b, 0)),
        ],
        out_specs=pl.BlockSpec((seq, HEAD_DIM), lambda b, h: (b, h)),
        out_shape=jax.ShapeDtypeStruct((n, WIDTH), BF16),
        scratch_shapes=[
            pltpu.VMEM((seq, 2 * HEAD_DIM), BF16),
            pltpu.VMEM((seq, 2 * HEAD_DIM), BF16),
        ],
        compiler_params=_params(("arbitrary", "arbitrary")),
        name="fox",
    )(proj, proj, proj, gate)


def _merge_kernel(oa_ref, oc_ref, bg_ref, cg_ref, hh_ref, ga_ref, gb_ref, gc_ref, h_ref,
                  convw_ref, wa_ref, wb_ref, wc_ref, wo_ref, out_ref, halo_ref, xs_ref,
                  *, tiles_per_seq):
    tile = h_ref.shape[0]

    @pl.when(pl.program_id(0) % tiles_per_seq == 0)
    def _():
        halo_ref[...] = jnp.zeros_like(halo_ref)

    _conv_stage(cg_ref[...].astype(F32) * hh_ref[...].astype(F32), halo_ref, xs_ref)
    conv = _conv_apply(xs_ref, convw_ref[...])
    sc = (bg_ref[...].astype(F32) * conv).astype(BF16)

    mix = _sigmoid(ga_ref[...].astype(F32)) * _dot(oa_ref[...], wa_ref[...])
    mix = mix + _sigmoid(gb_ref[...].astype(F32)) * _dot(sc, wb_ref[...])
    mix = mix + _sigmoid(gc_ref[...].astype(F32)) * _dot(oc_ref[...], wc_ref[...])
    out_ref[...] = h_ref[...] + _dot(mix.astype(BF16), wo_ref[...])


def _merge(oa, oc, proj, h, conv_sc, w_a, w_b, w_c, w_o, layer, tm, seq):
    n = h.shape[0]
    bch = COL_BCH // WIDTH
    gates = COL_GATES // D_MODEL
    tok = lambda width, blk: pl.BlockSpec((tm, width), lambda i: (i, blk))
    return pl.pallas_call(
        functools.partial(_merge_kernel, tiles_per_seq=seq // tm),
        grid=(n // tm,),
        in_specs=[
            tok(WIDTH, 0), tok(WIDTH, 0),
            tok(WIDTH, bch), tok(WIDTH, bch + 1), tok(WIDTH, bch + 2),
            tok(D_MODEL, gates), tok(D_MODEL, gates + 1), tok(D_MODEL, gates + 2),
            tok(D_MODEL, 0),
            _resident((SC_CONV, WIDTH)),
            _layer_resident((WIDTH, D_MODEL), layer), _layer_resident((WIDTH, D_MODEL), layer),
            _layer_resident((WIDTH, D_MODEL), layer), _layer_resident((D_MODEL, D_MODEL), layer),
        ],
        out_specs=tok(D_MODEL, 0),
        out_shape=jax.ShapeDtypeStruct((n, D_MODEL), F32),
        scratch_shapes=[
            pltpu.VMEM((ROW_HALO, WIDTH), F32),
            pltpu.VMEM((ROW_HALO + tm, WIDTH), F32),
        ],
        compiler_params=_params(("arbitrary",)),
        name="merge",
    )(oa, oc, proj, proj, proj, proj, proj, proj, h, conv_sc, w_a, w_b, w_c, w_o)


def _ffn_kernel(h_ref, g_ref, wup_ref, convw_ref, wdown_ref, gf_ref, out_ref, halo_ref, xs_ref, act_ref,
                *, tiles_per_seq, final_norm):
    tile = h_ref.shape[0]
    tf = FFN_TILE

    @pl.when(pl.program_id(0) % tiles_per_seq == 0)
    def _():
        halo_ref[...] = jnp.zeros_like(halo_ref)

    x = h_ref[...]
    xn = _rms(x, g_ref[...]).astype(BF16)
    def stage(j):
        for part in range(2):
            c0 = part * D_FF + j * tf
            _conv_stage(_dot(xn, wup_ref[:, c0:c0 + tf]), halo_ref.at[2 * j + part], xs_ref.at[2 * j + part])

    def finish(j):
        gate, up = (_conv_apply(xs_ref.at[2 * j + part], convw_ref[:, part * D_FF + j * tf:part * D_FF + (j + 1) * tf])
                    for part in range(2))
        act_ref[:, j * tf:(j + 1) * tf] = (gate * _sigmoid(gate) * up).astype(BF16)

    n_tiles = D_FF // tf
    stage(0)
    for j in range(n_tiles):
        if j + 1 < n_tiles:
            stage(j + 1)
        finish(j)
    y = x + _dot(act_ref[...], wdown_ref[...])
    if final_norm:
        y = _rms(y, gf_ref[...])
    out_ref[...] = y


def _ffn(h, g, w_up, conv_ffn, w_down, g_final, layer, tm, seq, final_norm):
    n = h.shape[0]
    return pl.pallas_call(
        functools.partial(_ffn_kernel, tiles_per_seq=seq // tm, final_norm=final_norm),
        grid=(n // tm,),
        in_specs=[
            pl.BlockSpec((tm, D_MODEL), lambda i: (i, 0)),
            _resident((1, D_MODEL)),
            _layer_resident((D_MODEL, 2 * D_FF), layer),
            _resident((FFN_CONV, 2 * D_FF)),
            _layer_resident((D_FF, D_MODEL), layer),
            _resident((1, D_MODEL)),
        ],
        out_specs=pl.BlockSpec((tm, D_MODEL), lambda i: (i, 0)),
        out_shape=jax.ShapeDtypeStruct((n, D_MODEL), F32),
        scratch_shapes=[
            pltpu.VMEM((2 * (D_FF // FFN_TILE), ROW_HALO, FFN_TILE), F32),
            pltpu.VMEM((2 * (D_FF // FFN_TILE), ROW_HALO + tm, FFN_TILE), F32),
            pltpu.VMEM((tm, D_FF), BF16),
        ],
        compiler_params=_params(("arbitrary",)),
        name="ffn",
    )(h, g, w_up, conv_ffn, w_down, g_final)


def _reorder_kernel(main_ref, next_ref, s1_ref, s2_ref, big_ref, small_ref):
    j = pl.program_id(1)
    cat = jnp.concatenate([main_ref[0], next_ref[0]], axis=1)

    def emit(shift):
        big_ref[0] = cat[:, shift:shift + D_MODEL].astype(BF16)

    pl.when(j < COL_BCH // D_MODEL)(lambda: emit(0))
    pl.when((j >= COL_BCH // D_MODEL) & (j < COL_GATES // D_MODEL))(lambda: emit(2 * N_HEADS))
    pl.when(j >= COL_GATES // D_MODEL)(lambda: emit(3 * N_HEADS))
    lane = lax.broadcasted_iota(jnp.int32, s1_ref.shape[1:], 1)
    small_ref[0] = jnp.where(lane < LANE_F, s1_ref[0], jnp.where(lane < LANE_F + N_HEADS, s2_ref[0], 0.0)).astype(BF16)


def _reorder_w_in(w_in):
    depth, d, width = w_in.shape
    scalars_1 = COL_BCH
    scalars_2 = COL_GATES + 2 * N_HEADS
    assert width == PROJ_COLS + 3 * N_HEADS and COL_BCH % D_MODEL == 0 and COL_GATES % D_MODEL == 0
    assert scalars_1 % SMALL_COLS == LANE_BETA and scalars_2 % SMALL_COLS == LANE_F and LANE_F == 2 * N_HEADS
    per_block = D_MODEL // SMALL_COLS
    return pl.pallas_call(
        _reorder_kernel,
        grid=(depth, PROJ_COLS // D_MODEL),
        in_specs=[
            pl.BlockSpec((1, d, D_MODEL), lambda l, j: (l, 0, j)),
            pl.BlockSpec((1, d, SMALL_COLS), lambda l, j: (l, 0, (j + 1) * per_block)),
            pl.BlockSpec((1, d, SMALL_COLS), lambda l, j: (l, 0, scalars_1 // SMALL_COLS)),
            pl.BlockSpec((1, d, SMALL_COLS), lambda l, j: (l, 0, scalars_2 // SMALL_COLS)),
        ],
        out_specs=[
            pl.BlockSpec((1, d, D_MODEL), lambda l, j: (l, 0, j)),
            pl.BlockSpec((1, d, SMALL_COLS), lambda l, j: (l, 0, 0)),
        ],
        out_shape=[
            jax.ShapeDtypeStruct((depth, d, PROJ_COLS), BF16),
            jax.ShapeDtypeStruct((depth, d, SMALL_COLS), BF16),
        ],
        compiler_params=_params(("arbitrary", "arbitrary")),
        name="reorder_w_in",
    )(w_in, w_in, w_in, w_in)


def _lane_row(pairs):
    row = jnp.zeros((1, SMALL_COLS), F32)
    for lane0, vals in pairs:
        row = row.at[0, lane0:lane0 + N_HEADS].set(vals.astype(F32))
    return row


def kernel(x, norm1_g, w_in, conv_qkv, a_log, dt_bias, gdn_norm, w_br_a, conv_sc, w_br_b, fox_bias, w_br_c, w_o, norm2_g, w_up, conv_ffn, w_down, norm_f):
    batch, seq, d = x.shape
    assert d == D_MODEL and seq % TOKEN_TILE == 0 and seq % GATES_TILE == 0 and seq % FOX_K_TILE == 0 and seq % MERGE_TILE == 0 and seq % GDN_TILE == 0
    tm = TOKEN_TILE
    h = x.reshape(batch * seq, d)
    w_big, w_small = _reorder_w_in(w_in)
    w_a, w_b, w_c, w_out = (w.astype(BF16) for w in (w_br_a, w_br_b, w_br_c, w_o))
    w_up16, w_down16 = w_up.astype(BF16), w_down.astype(BF16)
    for l in range(DEPTH):
        proj, small = _inproj(h, norm1_g[l][None, :], w_big, w_small, conv_qkv[l], l, tm, seq)
        bias_row = _lane_row([(LANE_G, dt_bias[l]), (LANE_F, fox_bias[l])])
        alog_row = _lane_row([(LANE_G, a_log[l])])
        gate = _gates(small, bias_row, alog_row, batch, seq)
        oa = _gdn(proj, gate, gdn_norm[l][None, :], batch, seq)
        oc = _fox(proj, gate, batch, seq)
        h = _merge(oa, oc, proj, h, conv_sc[l], w_a, w_b, w_c, w_out, l, MERGE_TILE, seq)
        h = _ffn(h, norm2_g[l][None, :], w_up16, conv_ffn[l], w_down16, norm_f[None, :], l, tm, seq,
                 final_norm=(l == DEPTH - 1))
    return h.reshape(batch, seq, d)
```

```python
import functools

import jax
import jax.numpy as jnp
from jax import lax
from jax.experimental import pallas as pl
from jax.experimental.pallas import tpu as pltpu

F32 = jnp.float32
BF16 = jnp.bfloat16

D_MODEL = 1024
DEPTH = 4
N_HEADS = 4
HEAD_DIM = 128
WIDTH = N_HEADS * HEAD_DIM
GDN_CONV = 4
GDN_CHUNK = 64
SC_CONV = 3
D_FF = 2816
FFN_CONV = 3
EPS = 1e-6

COL_QKV_A = 0
COL_Z_A = 3 * WIDTH
COL_BCH = 4 * WIDTH
COL_QKV_C = 7 * WIDTH
COL_GATES = 10 * WIDTH
PROJ_COLS = 10 * WIDTH + 3 * D_MODEL
LANE_BETA = 0
LANE_G = 4
LANE_F = 8
SMALL_COLS = 128

ROW_HALO = 8
TOKEN_TILE = 512
MERGE_TILE = 1024
GDN_TILE = 512
GATES_TILE = 512
QKV_TILE = 256
FOX_Q_TILE = 256
FOX_K_TILE = 256
FFN_TILE = 256
VMEM_LIMIT = 56 * 1024 * 1024


def _resident(shape):
    nd = len(shape)
    return pl.BlockSpec(shape, lambda *_: (0,) * nd, pipeline_mode=pl.Buffered(1))


def _layer_resident(shape, layer):
    nd = len(shape)
    return pl.BlockSpec((None,) + shape, lambda *_: (layer,) + (0,) * nd, pipeline_mode=pl.Buffered(1))


def _params(sem):
    return pltpu.CompilerParams(dimension_semantics=sem, vmem_limit_bytes=VMEM_LIMIT)


def _rms(x, g):
    return x * lax.rsqrt(jnp.mean(x * x, axis=-1, keepdims=True) + EPS) * g


LOG2_E = 1.4426950408889634


def _sigmoid(x):
    return 1.0 / (1.0 + jnp.exp2(x * -LOG2_E))


def _dot(a, b):
    return jnp.dot(a, b, preferred_element_type=F32)


def _dot_nt(a, b):
    return lax.dot_general(a, b, (((1,), (1,)), ((), ())), preferred_element_type=F32)


def _dot_tn(a, b):
    return lax.dot_general(a, b, (((0,), (0,)), ((), ())), preferred_element_type=F32)


def _bdot(a, b):
    return lax.dot_general(a, b, (((2,), (1,)), ((0,), (0,))), preferred_element_type=F32)


def _bdot_nt(a, b):
    return lax.dot_general(a, b, (((2,), (2,)), ((0,), (0,))), preferred_element_type=F32)


def _split3(x):
    h1 = x.astype(BF16)
    r1 = x - h1.astype(F32)
    h2 = r1.astype(BF16)
    h3 = (r1 - h2.astype(F32)).astype(BF16)
    return h1, h2, h3


def _bdot_f32(a, b):
    a_hi = a.astype(BF16).astype(F32)
    b_hi = b.astype(BF16).astype(F32)
    lhs = jnp.concatenate([a_hi, a - a_hi, a_hi], axis=-1).astype(BF16)
    rhs = jnp.concatenate([b_hi, b_hi, b - b_hi], axis=1).astype(BF16)
    return _bdot(lhs, rhs)


def _diff_operands(c):
    shape = c.shape[:-1] + (HEAD_DIM,)
    lane = lax.broadcasted_iota(jnp.int32, shape, len(shape) - 1)
    c1, c2, c3 = (term.astype(F32) for term in _split3(jnp.broadcast_to(c, shape)))
    terms = jnp.where((lane == 0) | (lane == 3), c1, jnp.where((lane == 1) | (lane == 4), c2, c3))
    a = jnp.where(lane < 3, terms, jnp.where(lane < 6, 1.0, 0.0))
    b = jnp.where(lane < 3, 1.0, jnp.where(lane < 6, -terms, 0.0))
    return a.astype(BF16), b.astype(BF16)


def _conv_stage(pre, halo_ref, xs_ref):
    tile = pre.shape[0]
    xs_ref[0:ROW_HALO, :] = halo_ref[...]
    xs_ref[ROW_HALO:ROW_HALO + tile, :] = pre
    halo_ref[...] = pre[tile - ROW_HALO:tile, :]


def _conv_apply(xs_ref, taps):
    ntap = taps.shape[0]
    staged = xs_ref[...]
    prev = pltpu.roll(staged, 1, axis=0)
    out = None
    for pair in range((ntap + 1) // 2):
        k = ntap - 1 - 2 * pair
        term = staged * taps[k:k + 1, :]
        if k > 0:
            term = term + prev * taps[k - 1:k, :]
        if pair > 0:
            term = pltpu.roll(term, 2 * pair, axis=0)
        out = term if out is None else out + term
    return out[ROW_HALO:, :]


def _inproj_kernel(x_ref, g_ref, w_ref, ws_ref, convw_ref, proj_ref, small_ref, halo_ref, xs_ref,
                   *, tiles_per_seq):
    @pl.when(pl.program_id(0) % tiles_per_seq == 0)
    def _():
        halo_ref[...] = jnp.zeros_like(halo_ref)

    xn = _rms(x_ref[...], g_ref[...]).astype(BF16)

    def stage(t):
        c0 = COL_QKV_A + t * QKV_TILE
        _conv_stage(_dot(xn, w_ref[:, c0:c0 + QKV_TILE]), halo_ref.at[t], xs_ref.at[t])

    def epilogue(t):
        c0 = COL_QKV_A + t * QKV_TILE
        part = (t * QKV_TILE) // WIDTH
        y = _conv_apply(xs_ref.at[t], convw_ref[:, c0:c0 + QKV_TILE])
        y = y * _sigmoid(y)
        for lo in range(0, QKV_TILE, HEAD_DIM):
            yh = y[:, lo:lo + HEAD_DIM]
            if part < 2:
                yh = yh * lax.rsqrt(jnp.sum(yh * yh, axis=-1, keepdims=True) + EPS)
            if part == 0:
                yh = yh * (HEAD_DIM ** -0.5)
            proj_ref[:, c0 + lo:c0 + lo + HEAD_DIM] = yh.astype(BF16)

    plain = [(COL_Z_A, COL_BCH)] + [(c0, c0 + D_MODEL) for c0 in range(COL_BCH, PROJ_COLS, D_MODEL)]
    n_qkv = 3 * WIDTH // QKV_TILE
    stage(0)
    for t in range(max(n_qkv, len(plain))):
        if t + 1 < n_qkv:
            stage(t + 1)
        if t < len(plain):
            c0, c1 = plain[t]
            proj_ref[:, c0:c1] = _dot(xn, w_ref[:, c0:c1]).astype(BF16)
        if t < n_qkv:
            epilogue(t)
    small_ref[...] = _dot(xn, ws_ref[...])


def _inproj(h, g, w_big, w_small, conv_qkv, layer, tm, seq):
    n = h.shape[0]
    return pl.pallas_call(
        functools.partial(_inproj_kernel, tiles_per_seq=seq // tm),
        grid=(n // tm,),
        in_specs=[
            pl.BlockSpec((tm, D_MODEL), lambda i: (i, 0)),
            _resident((1, D_MODEL)),
            _layer_resident((D_MODEL, PROJ_COLS), layer),
            _layer_resident((D_MODEL, SMALL_COLS), layer),
            _resident((GDN_CONV, 3 * WIDTH)),
        ],
        out_specs=[
            pl.BlockSpec((tm, PROJ_COLS), lambda i: (i, 0)),
            pl.BlockSpec((tm, SMALL_COLS), lambda i: (i, 0)),
        ],
        out_shape=[
            jax.ShapeDtypeStruct((n, PROJ_COLS), BF16),
            jax.ShapeDtypeStruct((n, SMALL_COLS), F32),
        ],
        scratch_shapes=[
            pltpu.VMEM((3 * WIDTH // QKV_TILE, ROW_HALO, QKV_TILE), F32),
            pltpu.VMEM((3 * WIDTH // QKV_TILE, ROW_HALO + tm, QKV_TILE), F32),
        ],
        compiler_params=_params(("arbitrary",)),
        name="inproj",
    )(h, g, w_big, w_small, conv_qkv)


def _gates_kernel(small_ref, bias_ref, alog_ref, out_ref, carry_ref):
    t = pl.program_id(1)
    tile = small_ref.shape[0]

    @pl.when(t == 0)
    def _():
        carry_ref[...] = jnp.zeros_like(carry_ref)

    x = small_ref[...] + bias_ref[...]
    lane = lax.broadcasted_iota(jnp.int32, x.shape, 1)
    soft = jnp.log1p(jnp.exp(-jnp.abs(x)))
    beta = _sigmoid(x)
    g = -jnp.exp(alog_ref[...]) * (jnp.maximum(x, 0.0) + soft)
    logf = jnp.minimum(x, 0.0) - soft
    vals = jnp.where(lane < LANE_G, beta, jnp.where(lane < LANE_F, g, logf))

    row = lax.broadcasted_iota(jnp.int32, (tile, tile), 0)
    col = lax.broadcasted_iota(jnp.int32, (tile, tile), 1)
    tril = row >= col
    m_full = tril.astype(BF16)
    m_seg = (tril & (row // GDN_CHUNK == col // GDN_CHUNK)).astype(BF16)
    cum = _dot(jnp.concatenate([m_full, m_seg], axis=0), jnp.concatenate(_split3(vals), axis=1))
    cum = cum[:, 0:SMALL_COLS] + (cum[:, SMALL_COLS:2 * SMALL_COLS] + cum[:, 2 * SMALL_COLS:3 * SMALL_COLS])
    cum_full = cum[0:tile] + carry_ref[0:1, :]
    cum_seg = cum[tile:2 * tile]
    carry_ref[0:1, :] = cum_full[tile - 1:tile, :]
    out_ref[...] = jnp.where(lane < LANE_G, beta, jnp.where(lane < LANE_F, cum_seg, cum_full))


def _gates(small, bias_row, alog_row, batch, seq):
    tile = GATES_TILE
    nt = seq // tile
    return pl.pallas_call(
        _gates_kernel,
        grid=(batch, nt),
        in_specs=[
            pl.BlockSpec((tile, SMALL_COLS), lambda b, t: (b * nt + t, 0)),
            _resident((1, SMALL_COLS)),
            _resident((1, SMALL_COLS)),
        ],
        out_specs=pl.BlockSpec((tile, SMALL_COLS), lambda b, t: (b * nt + t, 0)),
        out_shape=jax.ShapeDtypeStruct(small.shape, F32),
        scratch_shapes=[pltpu.VMEM((ROW_HALO, SMALL_COLS), F32)],
        compiler_params=_params(("arbitrary", "arbitrary")),
        name="gates",
    )(small, bias_row, alog_row)


def _unit_lower_inverse(l_strict):
    c = l_strict.shape[-1]
    row = lax.broadcasted_iota(jnp.int32, (c, c), 0)
    col = lax.broadcasted_iota(jnp.int32, (c, c), 1)
    power = -l_strict
    prod = jnp.where(row == col, 1.0, 0.0)[None] + power
    span = 1
    while 2 * span < c:
        power = _bdot_f32(power, power) if span == 1 else stacked[:, :c]
        span *= 2
        if 2 * span < c:
            stacked = _bdot_f32(jnp.concatenate([power, prod], axis=1), power)
            prod = prod + stacked[:, c:]
        else:
            prod = prod + _bdot_f32(prod, power)
    return prod


def _gdn_kernel(qkv_ref, z_ref, gate_ref, norm_ref, o_ref, state_ref):
    tile = qkv_ref.shape[0]

    @pl.when(pl.program_id(1) == 0)
    def _():
        state_ref[...] = jnp.zeros_like(state_ref)

    c = GDN_CHUNK
    nchunk = tile // c
    problems = [(j, h) for j in range(nchunk) for h in range(N_HEADS)]

    def slabs(col0):
        return jnp.stack([qkv_ref[j * c:(j + 1) * c, col0 + h * HEAD_DIM:col0 + (h + 1) * HEAD_DIM]
                          for j, h in problems])

    def lanes(lane0, rows=None):
        return jnp.stack([gate_ref[(j * c if rows is None else j * c + rows):(j + 1) * c,
                                   lane0 + h:lane0 + h + 1] for j, h in problems])

    q = slabs(0).astype(F32)
    k16 = slabs(WIDTH)
    k = k16.astype(F32)
    v = slabs(2 * WIDTH).astype(F32)
    beta = lanes(LANE_BETA)
    gc = lanes(LANE_G)
    g_last = lanes(LANE_G, rows=c - 1)

    row = lax.broadcasted_iota(jnp.int32, (c, c), 0)
    col = lax.broadcasted_iota(jnp.int32, (c, c), 1)
    da, db = _diff_operands(gc)
    decay = jnp.exp(jnp.where((row >= col)[None], _bdot_nt(da, db), -jnp.inf))
    kb = k * beta
    kk_qk = _bdot_nt(jnp.concatenate([kb.astype(BF16), slabs(0)], axis=1), k16)
    l_strict = jnp.where((row > col)[None], kk_qk[:, 0:c] * decay, 0.0)
    attn = (kk_qk[:, c:2 * c] * decay).astype(BF16)
    tm = _unit_lower_inverse(l_strict).astype(BF16)
    uw = _bdot(tm, jnp.concatenate([v * beta, kb * jnp.exp(gc)], axis=-1).astype(BF16))
    u = uw[..., 0:HEAD_DIM]
    w = uw[..., HEAD_DIM:2 * HEAD_DIM].astype(BF16)
    qg = (q * jnp.exp(gc)).astype(BF16)
    kd = (k * jnp.exp(g_last - gc)).astype(BF16)
    s_decay = jnp.exp(g_last)

    u_hi = u.astype(BF16)
    u_lo = (u - u_hi.astype(F32)).astype(BF16)
    wuu = jnp.concatenate([w, u_hi, u_lo], axis=-1)
    kd_t = jnp.stack([_dot_tn(kd[p], wuu[p]) for p in range(len(problems))])
    kd_w = kd_t[..., 0:HEAD_DIM]
    kd_u = kd_t[..., HEAD_DIM:2 * HEAD_DIM] + kd_t[..., 2 * HEAD_DIM:3 * HEAD_DIM]
    kd_w_hi = kd_w.astype(BF16)
    kd_w_split = jnp.concatenate([kd_w_hi, (kd_w - kd_w_hi.astype(F32)).astype(BF16)], axis=-1)

    s = state_ref[...]
    starts = []
    for j in range(nchunk):
        b0, b1 = j * N_HEADS, (j + 1) * N_HEADS
        s16 = s.astype(BF16)
        starts.append(s16)
        s = s * s_decay[b0:b1] + kd_u[b0:b1] - _bdot(kd_w_split[b0:b1], jnp.concatenate([s16, s16], axis=1))
    state_ref[...] = s

    s_start = jnp.concatenate(starts, axis=0)
    wq_s = _bdot(jnp.concatenate([w, qg], axis=1), s_start)
    v_new = (u - wq_s[:, 0:c]).astype(BF16)
    o = _rms(wq_s[:, c:2 * c] + _bdot(attn, v_new), norm_ref[...][None])
    for p, (j, h) in enumerate(problems):
        lo = h * HEAD_DIM
        z = z_ref[j * c:(j + 1) * c, lo:lo + HEAD_DIM].astype(F32)
        o_ref[j * c:(j + 1) * c, lo:lo + HEAD_DIM] = (o[p] * (z * _sigmoid(z))).astype(BF16)


def _gdn(proj, gate, gdn_norm, batch, seq):
    tile = GDN_TILE
    nt = seq // tile
    n = proj.shape[0]
    return pl.pallas_call(
        _gdn_kernel,
        grid=(batch, nt),
        in_specs=[
            pl.BlockSpec((tile, 3 * WIDTH), lambda b, t: (b * nt + t, COL_QKV_A // (3 * WIDTH))),
            pl.BlockSpec((tile, WIDTH), lambda b, t: (b * nt + t, COL_Z_A // WIDTH)),
            pl.BlockSpec((tile, SMALL_COLS), lambda b, t: (b * nt + t, 0)),
            _resident((1, HEAD_DIM)),
        ],
        out_specs=pl.BlockSpec((tile, WIDTH), lambda b, t: (b * nt + t, 0)),
        out_shape=jax.ShapeDtypeStruct((n, WIDTH), BF16),
        scratch_shapes=[pltpu.VMEM((N_HEADS, HEAD_DIM, HEAD_DIM), F32)],
        compiler_params=_params(("arbitrary", "arbitrary")),
        name="gdn",
    )(proj, proj, gate, gdn_norm)


def _fox_kernel(q_ref, k_ref, v_ref, gate_ref, o_ref, qa_ref, ka_ref, va_ref):
    seq = q_ref.shape[0]
    tq, tk = FOX_Q_TILE, FOX_K_TILE
    scale = HEAD_DIM ** -0.5
    lane = lax.broadcasted_iota(jnp.int32, (seq, SMALL_COLS), 1)
    c = jnp.sum(jnp.where(lane == LANE_F + pl.program_id(1), gate_ref[...], 0.0), axis=-1, keepdims=True)
    ca, cb = _diff_operands(c * (1.0 / scale))
    qa_ref[:, 0:HEAD_DIM] = q_ref[...]
    qa_ref[:, HEAD_DIM:2 * HEAD_DIM] = ca
    ka_ref[:, 0:HEAD_DIM] = k_ref[...]
    ka_ref[:, HEAD_DIM:2 * HEAD_DIM] = cb
    va_ref[:, 0:HEAD_DIM] = v_ref[...]
    va_ref[:, HEAD_DIM:2 * HEAD_DIM] = jnp.ones((seq, HEAD_DIM), BF16)
    row = lax.broadcasted_iota(jnp.int32, (tq, tk), 0)
    col = lax.broadcasted_iota(jnp.int32, (tq, tk), 1)
    log2e_scale = scale * LOG2_E
    n_q = seq // tq
    m = [jnp.full((tq, 1), -jnp.inf, F32) for _ in range(n_q)]
    acc = [jnp.zeros((tq, 2 * HEAD_DIM), F32) for _ in range(n_q)]
    for j in range(seq // tk):
        kblk = ka_ref[j * tk:(j + 1) * tk, :]
        vblk = va_ref[j * tk:(j + 1) * tk, :]
        for i in range((j * tk) // tq, n_q):
            s = _dot_nt(qa_ref[i * tq:(i + 1) * tq, :], kblk)
            if (i * tq) // tk == j:
                s = jnp.where(row + (i * tq - j * tk) >= col, s, -jnp.inf)
            m_new = jnp.maximum(m[i], jnp.max(s, axis=-1, keepdims=True))
            p = jnp.exp2((s - m_new) * log2e_scale)
            alpha = jnp.exp2((m[i] - m_new) * log2e_scale)
            acc[i] = alpha * acc[i] + _dot(p.astype(BF16), vblk)
            m[i] = m_new
            if (i * tq) // tk == j:
                o_ref[i * tq:(i + 1) * tq, :] = (acc[i][:, 0:HEAD_DIM] / acc[i][:, HEAD_DIM:2 * HEAD_DIM]).astype(BF16)


def _fox(proj, gate, batch, seq):
    n = proj.shape[0]
    blk = COL_QKV_C // HEAD_DIM
    return pl.pallas_call(
        _fox_kernel,
        grid=(batch, N_HEADS),
        in_specs=[
            pl.BlockSpec((seq, HEAD_DIM), lambda b, h: (b, blk + h)),
            pl.BlockSpec((seq, HEAD_DIM), lambda b, h: (b, blk + N_HEADS + h)),
            pl.BlockSpec((seq, HEAD_DIM), lambda b, h: (b, blk + 2 * N_HEADS + h)),
            pl.BlockSpec((seq, SMALL_COLS), lambda b, h: (b, 0)),
        ],
        out_specs=pl.BlockSpec((seq, HEAD_DIM), lambda b, h: (b, h)),
        out_shape=jax.ShapeDtypeStruct((n, WIDTH), BF16),
        scratch_shapes=[pltpu.VMEM((seq, 2 * HEAD_DIM), BF16)] * 3,
        compiler_params=_params(("arbitrary", "arbitrary")),
        name="fox",
    )(proj, proj, proj, gate)


def _merge_kernel(oa_ref, oc_ref, bg_ref, cg_ref, hh_ref, ga_ref, gb_ref, gc_ref, h_ref,
                  convw_ref, wa_ref, wb_ref, wc_ref, wo_ref, out_ref, halo_ref, xs_ref,
                  *, tiles_per_seq):
    tile = h_ref.shape[0]

    @pl.when(pl.program_id(0) % tiles_per_seq == 0)
    def _():
        halo_ref[...] = jnp.zeros_like(halo_ref)

    _conv_stage(cg_ref[...].astype(F32) * hh_ref[...].astype(F32), halo_ref, xs_ref)
    conv = _conv_apply(xs_ref, convw_ref[...])
    sc = (bg_ref[...].astype(F32) * conv).astype(BF16)

    mix = _sigmoid(ga_ref[...].astype(F32)) * _dot(oa_ref[...], wa_ref[...])
    mix = mix + _sigmoid(gb_ref[...].astype(F32)) * _dot(sc, wb_ref[...])
    mix = mix + _sigmoid(gc_ref[...].astype(F32)) * _dot(oc_ref[...], wc_ref[...])
    out_ref[...] = h_ref[...] + _dot(mix.astype(BF16), wo_ref[...])


def _merge(oa, oc, proj, h, conv_sc, w_a, w_b, w_c, w_o, layer, tm, seq):
    n = h.shape[0]
    bch = COL_BCH // WIDTH
    gates = COL_GATES // D_MODEL
    tok = lambda width, blk: pl.BlockSpec((tm, width), lambda i: (i, blk))
    return pl.pallas_call(
        functools.partial(_merge_kernel, tiles_per_seq=seq // tm),
        grid=(n // tm,),
        in_specs=[
            tok(WIDTH, 0), tok(WIDTH, 0),
            tok(WIDTH, bch), tok(WIDTH, bch + 1), tok(WIDTH, bch + 2),
            tok(D_MODEL, gates), tok(D_MODEL, gates + 1), tok(D_MODEL, gates + 2),
            tok(D_MODEL, 0),
            _resident((SC_CONV, WIDTH)),
            _layer_resident((WIDTH, D_MODEL), layer), _layer_resident((WIDTH, D_MODEL), layer),
            _layer_resident((WIDTH, D_MODEL), layer), _layer_resident((D_MODEL, D_MODEL), layer),
        ],
        out_specs=tok(D_MODEL, 0),
        out_shape=jax.ShapeDtypeStruct((n, D_MODEL), F32),
        scratch_shapes=[
            pltpu.VMEM((ROW_HALO, WIDTH), F32),
            pltpu.VMEM((ROW_HALO + tm, WIDTH), F32),
        ],
        compiler_params=_params(("arbitrary",)),
        name="merge",
    )(oa, oc, proj, proj, proj, proj, proj, proj, h, conv_sc, w_a, w_b, w_c, w_o)


def _ffn_kernel(h_ref, g_ref, wup_ref, convw_ref, wdown_ref, gf_ref, out_ref, halo_ref, xs_ref, act_ref,
                *, tiles_per_seq, final_norm):
    tile = h_ref.shape[0]
    tf = FFN_TILE

    @pl.when(pl.program_id(0) % tiles_per_seq == 0)
    def _():
        halo_ref[...] = jnp.zeros_like(halo_ref)

    x = h_ref[...]
    xn = _rms(x, g_ref[...]).astype(BF16)
    def stage(j):
        for part in range(2):
            c0 = part * D_FF + j * tf
            _conv_stage(_dot(xn, wup_ref[:, c0:c0 + tf]), halo_ref.at[2 * j + part], xs_ref.at[2 * j + part])

    def finish(j):
        gate, up = (_conv_apply(xs_ref.at[2 * j + part], convw_ref[:, part * D_FF + j * tf:part * D_FF + (j + 1) * tf])
                    for part in range(2))
        act_ref[:, j * tf:(j + 1) * tf] = (gate * _sigmoid(gate) * up).astype(BF16)

    n_tiles = D_FF // tf
    stage(0)
    for j in range(n_tiles):
        if j + 1 < n_tiles:
            stage(j + 1)
        finish(j)
    y = x + _dot(act_ref[...], wdown_ref[...])
    if final_norm:
        y = _rms(y, gf_ref[...])
    out_ref[...] = y


def _ffn(h, g, w_up, conv_ffn, w_down, g_final, layer, tm, seq, final_norm):
    n = h.shape[0]
    return pl.pallas_call(
        functools.partial(_ffn_kernel, tiles_per_seq=seq // tm, final_norm=final_norm),
        grid=(n // tm,),
        in_specs=[
            pl.BlockSpec((tm, D_MODEL), lambda i: (i, 0)),
            _resident((1, D_MODEL)),
            _layer_resident((D_MODEL, 2 * D_FF), layer),
            _resident((FFN_CONV, 2 * D_FF)),
            _layer_resident((D_FF, D_MODEL), layer),
            _resident((1, D_MODEL)),
        ],
        out_specs=pl.BlockSpec((tm, D_MODEL), lambda i: (i, 0)),
        out_shape=jax.ShapeDtypeStruct((n, D_MODEL), F32),
        scratch_shapes=[
            pltpu.VMEM((2 * (D_FF // FFN_TILE), ROW_HALO, FFN_TILE), F32),
            pltpu.VMEM((2 * (D_FF // FFN_TILE), ROW_HALO + tm, FFN_TILE), F32),
            pltpu.VMEM((tm, D_FF), BF16),
        ],
        compiler_params=_params(("arbitrary",)),
        name="ffn",
    )(h, g, w_up, conv_ffn, w_down, g_final)


def _reorder_kernel(main_ref, next_ref, s1_ref, s2_ref, big_ref, small_ref):
    j = pl.program_id(1)
    cat = jnp.concatenate([main_ref[0], next_ref[0]], axis=1)

    def emit(shift):
        big_ref[0] = cat[:, shift:shift + D_MODEL].astype(BF16)

    pl.when(j < COL_BCH // D_MODEL)(lambda: emit(0))
    pl.when((j >= COL_BCH // D_MODEL) & (j < COL_GATES // D_MODEL))(lambda: emit(2 * N_HEADS))
    pl.when(j >= COL_GATES // D_MODEL)(lambda: emit(3 * N_HEADS))
    lane = lax.broadcasted_iota(jnp.int32, s1_ref.shape[1:], 1)
    small_ref[0] = jnp.where(lane < LANE_F, s1_ref[0], jnp.where(lane < LANE_F + N_HEADS, s2_ref[0], 0.0)).astype(BF16)


def _reorder_w_in(w_in):
    depth, d, width = w_in.shape
    scalars_1 = COL_BCH
    scalars_2 = COL_GATES + 2 * N_HEADS
    assert width == PROJ_COLS + 3 * N_HEADS and COL_BCH % D_MODEL == 0 and COL_GATES % D_MODEL == 0
    assert scalars_1 % SMALL_COLS == LANE_BETA and scalars_2 % SMALL_COLS == LANE_F and LANE_F == 2 * N_HEADS
    per_block = D_MODEL // SMALL_COLS
    return pl.pallas_call(
        _reorder_kernel,
        grid=(depth, PROJ_COLS // D_MODEL),
        in_specs=[
            pl.BlockSpec((1, d, D_MODEL), lambda l, j: (l, 0, j)),
            pl.BlockSpec((1, d, SMALL_COLS), lambda l, j: (l, 0, (j + 1) * per_block)),
            pl.BlockSpec((1, d, SMALL_COLS), lambda l, j: (l, 0, scalars_1 // SMALL_COLS)),
            pl.BlockSpec((1, d, SMALL_COLS), lambda l, j: (l, 0, scalars_2 // SMALL_COLS)),
        ],
        out_specs=[
            pl.BlockSpec((1, d, D_MODEL), lambda l, j: (l, 0, j)),
            pl.BlockSpec((1, d, SMALL_COLS), lambda l, j: (l, 0, 0)),
        ],
        out_shape=[
            jax.ShapeDtypeStruct((depth, d, PROJ_COLS), BF16),
            jax.ShapeDtypeStruct((depth, d, SMALL_COLS), BF16),
        ],
        compiler_params=_params(("arbitrary", "arbitrary")),
        name="reorder_w_in",
    )(w_in, w_in, w_in, w_in)


def _lane_row(pairs):
    row = jnp.zeros((1, SMALL_COLS), F32)
    for lane0, vals in pairs:
        row = row.at[0, lane0:lane0 + N_HEADS].set(vals.astype(F32))
    return row


def kernel(x, norm1_g, w_in, conv_qkv, a_log, dt_bias, gdn_norm, w_br_a, conv_sc, w_br_b, fox_bias, w_br_c, w_o, norm2_g, w_up, conv_ffn, w_down, norm_f):
    batch, seq, d = x.shape
    assert d == D_MODEL and seq % TOKEN_TILE == 0 and seq % GATES_TILE == 0 and seq % FOX_K_TILE == 0 and seq % MERGE_TILE == 0 and seq % GDN_TILE == 0
    tm = TOKEN_TILE
    h = x.reshape(batch * seq, d)
    w_big, w_small = _reorder_w_in(w_in)
    w_a, w_b, w_c, w_out = (w.astype(BF16) for w in (w_br_a, w_br_b, w_br_c, w_o))
    w_up16, w_down16 = w_up.astype(BF16), w_down.astype(BF16)
    for l in range(DEPTH):
        proj, small = _inproj(h, norm1_g[l][None, :], w_big, w_small, conv_qkv[l], l, tm, seq)
        bias_row = _lane_row([(LANE_G, dt_bias[l]), (LANE_F, fox_bias[l])])
        alog_row = _lane_row([(LANE_G, a_log[l])])
        gate = _gates(small, bias_row, alog_row, batch, seq)
        oa = _gdn(proj, gate, gdn_norm[l][None, :], batch, seq)
        oc = _fox(proj, gate, batch, seq)
        h = _merge(oa, oc, proj, h, conv_sc[l], w_a, w_b, w_c, w_out, l, MERGE_TILE, seq)
        h = _ffn(h, norm2_g[l][None, :], w_up16, conv_ffn[l], w_down16, norm_f[None, :], l, tm, seq,
                 final_norm=(l == DEPTH - 1))
    return h.reshape(batch, seq, d)
```

```python
import functools

import jax
import jax.numpy as jnp
from jax import lax
from jax.experimental import pallas as pl
from jax.experimental.pallas import tpu as pltpu

F32 = jnp.float32
BF16 = jnp.bfloat16

D_MODEL = 1024
DEPTH = 4
N_HEADS = 4
HEAD_DIM = 128
WIDTH = N_HEADS * HEAD_DIM
GDN_CONV = 4
GDN_CHUNK = 64
SC_CONV = 3
D_FF = 2816
FFN_CONV = 3
EPS = 1e-6

COL_QKV_A = 0
COL_Z_A = 3 * WIDTH
COL_BCH = 4 * WIDTH
COL_QKV_C = 7 * WIDTH
COL_GATES = 10 * WIDTH
PROJ_COLS = 10 * WIDTH + 3 * D_MODEL
LANE_BETA = 0
LANE_G = 4
LANE_F = 8
SMALL_COLS = 128

ROW_HALO = 8
TOKEN_TILE = 512
MERGE_TILE = 1024
GDN_TILE = 512
GATES_TILE = 512
QKV_TILE = 256
FOX_Q_TILE = 256
FOX_K_TILE = 256
FFN_TILE = 256
VMEM_LIMIT = 56 * 1024 * 1024


def _resident(shape):
    nd = len(shape)
    return pl.BlockSpec(shape, lambda *_: (0,) * nd, pipeline_mode=pl.Buffered(1))


def _layer_resident(shape, layer):
    nd = len(shape)
    return pl.BlockSpec((None,) + shape, lambda *_: (layer,) + (0,) * nd, pipeline_mode=pl.Buffered(1))


def _params(sem):
    return pltpu.CompilerParams(dimension_semantics=sem, vmem_limit_bytes=VMEM_LIMIT)


def _rms(x, g):
    return x * lax.rsqrt(jnp.mean(x * x, axis=-1, keepdims=True) + EPS) * g


LOG2_E = 1.4426950408889634
FOX_Q_SCALE = HEAD_DIM ** -0.5 * LOG2_E


def _sigmoid(x):
    return 1.0 / (1.0 + jnp.exp2(x * -LOG2_E))


def _dot(a, b):
    return jnp.dot(a, b, preferred_element_type=F32)


def _dot_nt(a, b):
    return lax.dot_general(a, b, (((1,), (1,)), ((), ())), preferred_element_type=F32)


def _dot_tn(a, b):
    return lax.dot_general(a, b, (((0,), (0,)), ((), ())), preferred_element_type=F32)


def _bdot(a, b):
    return lax.dot_general(a, b, (((2,), (1,)), ((0,), (0,))), preferred_element_type=F32)


def _bdot_nt(a, b):
    return lax.dot_general(a, b, (((2,), (2,)), ((0,), (0,))), preferred_element_type=F32)


def _split3(x):
    h1 = x.astype(BF16)
    r1 = x - h1.astype(F32)
    h2 = r1.astype(BF16)
    h3 = (r1 - h2.astype(F32)).astype(BF16)
    return h1, h2, h3


def _bdot_f32(a, b):
    a_hi = a.astype(BF16).astype(F32)
    b_hi = b.astype(BF16).astype(F32)
    lhs = jnp.concatenate([a_hi, a - a_hi, a_hi], axis=-1).astype(BF16)
    rhs = jnp.concatenate([b_hi, b_hi, b - b_hi], axis=1).astype(BF16)
    return _bdot(lhs, rhs)


def _diff_operands(c):
    shape = c.shape[:-1] + (HEAD_DIM,)
    lane = lax.broadcasted_iota(jnp.int32, shape, len(shape) - 1)
    c1, c2, c3 = (term.astype(F32) for term in _split3(jnp.broadcast_to(c, shape)))
    terms = jnp.where((lane == 0) | (lane == 3), c1, jnp.where((lane == 1) | (lane == 4), c2, c3))
    a = jnp.where(lane < 3, terms, jnp.where(lane < 6, 1.0, 0.0))
    b = jnp.where(lane < 3, 1.0, jnp.where(lane < 6, -terms, 0.0))
    return a.astype(BF16), b.astype(BF16)


def _conv_stage(pre, halo_ref, xs_ref):
    tile = pre.shape[0]
    xs_ref[0:ROW_HALO, :] = halo_ref[...]
    xs_ref[ROW_HALO:ROW_HALO + tile, :] = pre
    halo_ref[...] = pre[tile - ROW_HALO:tile, :]


def _conv_apply(xs_ref, taps):
    ntap = taps.shape[0]
    staged = xs_ref[...]
    prev = pltpu.roll(staged, 1, axis=0)
    out = None
    for pair in range((ntap + 1) // 2):
        k = ntap - 1 - 2 * pair
        term = staged * taps[k:k + 1, :]
        if k > 0:
            term = term + prev * taps[k - 1:k, :]
        if pair > 0:
            term = pltpu.roll(term, 2 * pair, axis=0)
        out = term if out is None else out + term
    return out[ROW_HALO:, :]


def _inproj_kernel(x_ref, g_ref, w_ref, ws_ref, convw_ref, proj_ref, small_ref, halo_ref, xs_ref,
                   *, tiles_per_seq):
    @pl.when(pl.program_id(0) % tiles_per_seq == 0)
    def _():
        halo_ref[...] = jnp.zeros_like(halo_ref)

    xn = _rms(x_ref[...], g_ref[...]).astype(BF16)

    def stage(t):
        c0 = COL_QKV_A + t * QKV_TILE
        _conv_stage(_dot(xn, w_ref[:, c0:c0 + QKV_TILE]), halo_ref.at[t], xs_ref.at[t])

    def epilogue(t):
        c0 = COL_QKV_A + t * QKV_TILE
        part = (t * QKV_TILE) // WIDTH
        y = _conv_apply(xs_ref.at[t], convw_ref[:, c0:c0 + QKV_TILE])
        y = y * _sigmoid(y)
        for lo in range(0, QKV_TILE, HEAD_DIM):
            yh = y[:, lo:lo + HEAD_DIM]
            if part < 2:
                yh = yh * lax.rsqrt(jnp.sum(yh * yh, axis=-1, keepdims=True) + EPS)
            if part == 0:
                yh = yh * (HEAD_DIM ** -0.5)
            proj_ref[:, c0 + lo:c0 + lo + HEAD_DIM] = yh.astype(BF16)

    plain = [(COL_Z_A, COL_BCH)] + [(c0, c0 + D_MODEL) for c0 in range(COL_BCH, PROJ_COLS, D_MODEL)]
    n_qkv = 3 * WIDTH // QKV_TILE
    stage(0)
    for t in range(max(n_qkv, len(plain))):
        if t + 1 < n_qkv:
            stage(t + 1)
        if t < len(plain):
            c0, c1 = plain[t]
            y = _dot(xn, w_ref[:, c0:c1])
            if c0 <= COL_QKV_C < c1:
                lo = COL_QKV_C - c0
                assert lo + WIDTH <= c1 - c0
                pieces = [y[:, 0:lo], y[:, lo:lo + WIDTH] * FOX_Q_SCALE, y[:, lo + WIDTH:]]
                y = jnp.concatenate([piece for piece in pieces if piece.shape[1] > 0], axis=1)
            proj_ref[:, c0:c1] = y.astype(BF16)
        if t < n_qkv:
            epilogue(t)
    small_ref[...] = _dot(xn, ws_ref[...])


def _inproj(h, g, w_big, w_small, conv_qkv, layer, tm, seq):
    n = h.shape[0]
    return pl.pallas_call(
        functools.partial(_inproj_kernel, tiles_per_seq=seq // tm),
        grid=(n // tm,),
        in_specs=[
            pl.BlockSpec((tm, D_MODEL), lambda i: (i, 0)),
            _resident((1, D_MODEL)),
            _layer_resident((D_MODEL, PROJ_COLS), layer),
            _layer_resident((D_MODEL, SMALL_COLS), layer),
            _resident((GDN_CONV, 3 * WIDTH)),
        ],
        out_specs=[
            pl.BlockSpec((tm, PROJ_COLS), lambda i: (i, 0)),
            pl.BlockSpec((tm, SMALL_COLS), lambda i: (i, 0)),
        ],
        out_shape=[
            jax.ShapeDtypeStruct((n, PROJ_COLS), BF16),
            jax.ShapeDtypeStruct((n, SMALL_COLS), F32),
        ],
        scratch_shapes=[
            pltpu.VMEM((3 * WIDTH // QKV_TILE, ROW_HALO, QKV_TILE), F32),
            pltpu.VMEM((3 * WIDTH // QKV_TILE, ROW_HALO + tm, QKV_TILE), F32),
        ],
        compiler_params=_params(("arbitrary",)),
        name="inproj",
    )(h, g, w_big, w_small, conv_qkv)


def _gates_kernel(small_ref, bias_ref, alog_ref, out_ref, carry_ref):
    t = pl.program_id(1)
    tile = small_ref.shape[0]

    @pl.when(t == 0)
    def _():
        carry_ref[...] = jnp.zeros_like(carry_ref)

    x = small_ref[...] + bias_ref[...]
    lane = lax.broadcasted_iota(jnp.int32, x.shape, 1)
    soft = jnp.log1p(jnp.exp(-jnp.abs(x)))
    beta = _sigmoid(x)
    g = -jnp.exp(alog_ref[...]) * (jnp.maximum(x, 0.0) + soft)
    logf = jnp.minimum(x, 0.0) - soft
    vals = jnp.where(lane < LANE_G, beta, jnp.where(lane < LANE_F, g, logf))

    row = lax.broadcasted_iota(jnp.int32, (tile, tile), 0)
    col = lax.broadcasted_iota(jnp.int32, (tile, tile), 1)
    tril = row >= col
    m_full = tril.astype(BF16)
    m_seg = (tril & (row // GDN_CHUNK == col // GDN_CHUNK)).astype(BF16)
    cum = _dot(jnp.concatenate([m_full, m_seg], axis=0), jnp.concatenate(_split3(vals), axis=1))
    cum = cum[:, 0:SMALL_COLS] + (cum[:, SMALL_COLS:2 * SMALL_COLS] + cum[:, 2 * SMALL_COLS:3 * SMALL_COLS])
    cum_full = cum[0:tile] + carry_ref[0:1, :]
    cum_seg = cum[tile:2 * tile]
    carry_ref[0:1, :] = cum_full[tile - 1:tile, :]
    out_ref[...] = jnp.where(lane < LANE_G, beta, jnp.where(lane < LANE_F, cum_seg, cum_full))


def _gates(small, bias_row, alog_row, batch, seq):
    tile = GATES_TILE
    nt = seq // tile
    return pl.pallas_call(
        _gates_kernel,
        grid=(batch, nt),
        in_specs=[
            pl.BlockSpec((tile, SMALL_COLS), lambda b, t: (b * nt + t, 0)),
            _resident((1, SMALL_COLS)),
            _resident((1, SMALL_COLS)),
        ],
        out_specs=pl.BlockSpec((tile, SMALL_COLS), lambda b, t: (b * nt + t, 0)),
        out_shape=jax.ShapeDtypeStruct(small.shape, F32),
        scratch_shapes=[pltpu.VMEM((ROW_HALO, SMALL_COLS), F32)],
        compiler_params=_params(("arbitrary", "arbitrary")),
        name="gates",
    )(small, bias_row, alog_row)


def _unit_lower_inverse(l_strict):
    c = l_strict.shape[-1]
    row = lax.broadcasted_iota(jnp.int32, (c, c), 0)
    col = lax.broadcasted_iota(jnp.int32, (c, c), 1)
    power = -l_strict
    prod = jnp.where(row == col, 1.0, 0.0)[None] + power
    span = 1
    while 2 * span < c:
        power = _bdot_f32(power, power) if span == 1 else stacked[:, :c]
        span *= 2
        if 2 * span < c:
            stacked = _bdot_f32(jnp.concatenate([power, prod], axis=1), power)
            prod = prod + stacked[:, c:]
        else:
            prod = prod + _bdot_f32(prod, power)
    return prod


def _gdn_kernel(qkv_ref, z_ref, gate_ref, norm_ref, o_ref, state_ref):
    tile = qkv_ref.shape[0]

    @pl.when(pl.program_id(1) == 0)
    def _():
        state_ref[...] = jnp.zeros_like(state_ref)

    c = GDN_CHUNK
    nchunk = tile // c
    problems = [(j, h) for j in range(nchunk) for h in range(N_HEADS)]

    def slabs(col0):
        return jnp.stack([qkv_ref[j * c:(j + 1) * c, col0 + h * HEAD_DIM:col0 + (h + 1) * HEAD_DIM]
                          for j, h in problems])

    def lanes(lane0, rows=None):
        return jnp.stack([gate_ref[(j * c if rows is None else j * c + rows):(j + 1) * c,
                                   lane0 + h:lane0 + h + 1] for j, h in problems])

    q = slabs(0).astype(F32)
    k16 = slabs(WIDTH)
    k = k16.astype(F32)
    v = slabs(2 * WIDTH).astype(F32)
    beta = lanes(LANE_BETA)
    gc = lanes(LANE_G)
    g_last = lanes(LANE_G, rows=c - 1)

    row = lax.broadcasted_iota(jnp.int32, (c, c), 0)
    col = lax.broadcasted_iota(jnp.int32, (c, c), 1)
    da, db = _diff_operands(gc)
    decay = jnp.exp(jnp.where((row >= col)[None], _bdot_nt(da, db), -jnp.inf))
    kb = k * beta
    kk_qk = _bdot_nt(jnp.concatenate([kb.astype(BF16), slabs(0)], axis=1), k16)
    l_strict = jnp.where((row > col)[None], kk_qk[:, 0:c] * decay, 0.0)
    attn = (kk_qk[:, c:2 * c] * decay).astype(BF16)
    tm = _unit_lower_inverse(l_strict).astype(BF16)
    uw = _bdot(tm, jnp.concatenate([v * beta, kb * jnp.exp(gc)], axis=-1).astype(BF16))
    u = uw[..., 0:HEAD_DIM]
    w = uw[..., HEAD_DIM:2 * HEAD_DIM].astype(BF16)
    qg = (q * jnp.exp(gc)).astype(BF16)
    kd = (k * jnp.exp(g_last - gc)).astype(BF16)
    s_decay = jnp.exp(g_last)

    u_hi = u.astype(BF16)
    u_lo = (u - u_hi.astype(F32)).astype(BF16)
    wuu = jnp.concatenate([w, u_hi, u_lo], axis=-1)
    kd_t = jnp.stack([_dot_tn(kd[p], wuu[p]) for p in range(len(problems))])
    kd_w = kd_t[..., 0:HEAD_DIM]
    kd_u = kd_t[..., HEAD_DIM:2 * HEAD_DIM] + kd_t[..., 2 * HEAD_DIM:3 * HEAD_DIM]
    kd_w_hi = kd_w.astype(BF16)
    kd_w_split = jnp.concatenate([kd_w_hi, (kd_w - kd_w_hi.astype(F32)).astype(BF16)], axis=-1)

    s = state_ref[...]
    starts = []
    for j in range(nchunk):
        b0, b1 = j * N_HEADS, (j + 1) * N_HEADS
        s16 = s.astype(BF16)
        starts.append(s16)
        s = s * s_decay[b0:b1] + kd_u[b0:b1] - _bdot(kd_w_split[b0:b1], jnp.concatenate([s16, s16], axis=1))
    state_ref[...] = s

    s_start = jnp.concatenate(starts, axis=0)
    wq_s = _bdot(jnp.concatenate([w, qg], axis=1), s_start)
    v_new = (u - wq_s[:, 0:c]).astype(BF16)
    o = _rms(wq_s[:, c:2 * c] + _bdot(attn, v_new), norm_ref[...][None])
    for p, (j, h) in enumerate(problems):
        lo = h * HEAD_DIM
        z = z_ref[j * c:(j + 1) * c, lo:lo + HEAD_DIM].astype(F32)
        o_ref[j * c:(j + 1) * c, lo:lo + HEAD_DIM] = (o[p] * (z * _sigmoid(z))).astype(BF16)


def _gdn(proj, gate, gdn_norm, batch, seq):
    tile = GDN_TILE
    nt = seq // tile
    n = proj.shape[0]
    return pl.pallas_call(
        _gdn_kernel,
        grid=(batch, nt),
        in_specs=[
            pl.BlockSpec((tile, 3 * WIDTH), lambda b, t: (b * nt + t, COL_QKV_A // (3 * WIDTH))),
            pl.BlockSpec((tile, WIDTH), lambda b, t: (b * nt + t, COL_Z_A // WIDTH)),
            pl.BlockSpec((tile, SMALL_COLS), lambda b, t: (b * nt + t, 0)),
            _resident((1, HEAD_DIM)),
        ],
        out_specs=pl.BlockSpec((tile, WIDTH), lambda b, t: (b * nt + t, 0)),
        out_shape=jax.ShapeDtypeStruct((n, WIDTH), BF16),
        scratch_shapes=[pltpu.VMEM((N_HEADS, HEAD_DIM, HEAD_DIM), F32)],
        compiler_params=_params(("arbitrary", "arbitrary")),
        name="gdn",
    )(proj, proj, gate, gdn_norm)


def _fox_kernel(q_ref, k_ref, v_ref, gate_ref, o_ref, qa_ref, ka_ref, va_ref):
    seq = q_ref.shape[0]
    tq, tk = FOX_Q_TILE, FOX_K_TILE
    lane = lax.broadcasted_iota(jnp.int32, (seq, SMALL_COLS), 1)
    c = jnp.sum(jnp.where(lane == LANE_F + pl.program_id(1), gate_ref[...], 0.0), axis=-1, keepdims=True)
    ca, cb = _diff_operands(c * LOG2_E)
    qa_ref[:, 0:HEAD_DIM] = q_ref[...]
    qa_ref[:, HEAD_DIM:2 * HEAD_DIM] = ca
    ka_ref[:, 0:HEAD_DIM] = k_ref[...]
    ka_ref[:, HEAD_DIM:2 * HEAD_DIM] = cb
    va_ref[:, 0:HEAD_DIM] = v_ref[...]
    va_ref[:, HEAD_DIM:2 * HEAD_DIM] = jnp.ones((seq, HEAD_DIM), BF16)
    row = lax.broadcasted_iota(jnp.int32, (tq, tk), 0)
    col = lax.broadcasted_iota(jnp.int32, (tq, tk), 1)
    n_q = seq // tq
    m = [jnp.full((tq, 1), -jnp.inf, F32) for _ in range(n_q)]
    acc = [jnp.zeros((tq, 2 * HEAD_DIM), F32) for _ in range(n_q)]
    for j in range(seq // tk):
        kblk = ka_ref[j * tk:(j + 1) * tk, :]
        vblk = va_ref[j * tk:(j + 1) * tk, :]
        for i in range((j * tk) // tq, n_q):
            s = _dot_nt(qa_ref[i * tq:(i + 1) * tq, :], kblk)
            if (i * tq) // tk == j:
                s = jnp.where(row + (i * tq - j * tk) >= col, s, -jnp.inf)
            m_new = jnp.maximum(m[i], jnp.max(s, axis=-1, keepdims=True))
            p = jnp.exp2(s - m_new)
            alpha = jnp.exp2(m[i] - m_new)
            acc[i] = alpha * acc[i] + _dot(p.astype(BF16), vblk)
            m[i] = m_new
            if (i * tq) // tk == j:
                o_ref[i * tq:(i + 1) * tq, :] = (acc[i][:, 0:HEAD_DIM] / acc[i][:, HEAD_DIM:2 * HEAD_DIM]).astype(BF16)


def _fox(proj, gate, batch, seq):
    n = proj.shape[0]
    blk = COL_QKV_C // HEAD_DIM
    return pl.pallas_call(
        _fox_kernel,
        grid=(batch, N_HEADS),
        in_specs=[
            pl.BlockSpec((seq, HEAD_DIM), lambda b, h: (b, blk + h)),
            pl.BlockSpec((seq, HEAD_DIM), lambda b, h: (b, blk + N_HEADS + h)),
            pl.BlockSpec((seq, HEAD_DIM), lambda b, h: (b, blk + 2 * N_HEADS + h)),
            pl.BlockSpec((seq, SMALL_COLS), lambda b, h: (b, 0)),
        ],
        out_specs=pl.BlockSpec((seq, HEAD_DIM), lambda b, h: (b, h)),
        out_shape=jax.ShapeDtypeStruct((n, WIDTH), BF16),
        scratch_shapes=[pltpu.VMEM((seq, 2 * HEAD_DIM), BF16)] * 3,
        compiler_params=_params(("arbitrary", "arbitrary")),
        name="fox",
    )(proj, proj, proj, gate)


def _merge_kernel(oa_ref, oc_ref, bg_ref, cg_ref, hh_ref, ga_ref, gb_ref, gc_ref, h_ref,
                  convw_ref, wa_ref, wb_ref, wc_ref, wo_ref, out_ref, halo_ref, xs_ref,
                  *, tiles_per_seq):
    tile = h_ref.shape[0]

    @pl.when(pl.program_id(0) % tiles_per_seq == 0)
    def _():
        halo_ref[...] = jnp.zeros_like(halo_ref)

    _conv_stage(cg_ref[...].astype(F32) * hh_ref[...].astype(F32), halo_ref, xs_ref)
    conv = _conv_apply(xs_ref, convw_ref[...])
    sc = (bg_ref[...].astype(F32) * conv).astype(BF16)

    mix = _sigmoid(ga_ref[...].astype(F32)) * _dot(oa_ref[...], wa_ref[...])
    mix = mix + _sigmoid(gb_ref[...].astype(F32)) * _dot(sc, wb_ref[...])
    mix = mix + _sigmoid(gc_ref[...].astype(F32)) * _dot(oc_ref[...], wc_ref[...])
    out_ref[...] = h_ref[...] + _dot(mix.astype(BF16), wo_ref[...])


def _merge(oa, oc, proj, h, conv_sc, w_a, w_b, w_c, w_o, layer, tm, seq):
    n = h.shape[0]
    bch = COL_BCH // WIDTH
    gates = COL_GATES // D_MODEL
    tok = lambda width, blk: pl.BlockSpec((tm, width), lambda i: (i, blk))
    return pl.pallas_call(
        functools.partial(_merge_kernel, tiles_per_seq=seq // tm),
        grid=(n // tm,),
        in_specs=[
            tok(WIDTH, 0), tok(WIDTH, 0),
            tok(WIDTH, bch), tok(WIDTH, bch + 1), tok(WIDTH, bch + 2),
            tok(D_MODEL, gates), tok(D_MODEL, gates + 1), tok(D_MODEL, gates + 2),
            tok(D_MODEL, 0),
            _resident((SC_CONV, WIDTH)),
            _layer_resident((WIDTH, D_MODEL), layer), _layer_resident((WIDTH, D_MODEL), layer),
            _layer_resident((WIDTH, D_MODEL), layer), _layer_resident((D_MODEL, D_MODEL), layer),
        ],
        out_specs=tok(D_MODEL, 0),
        out_shape=jax.ShapeDtypeStruct((n, D_MODEL), F32),
        scratch_shapes=[
            pltpu.VMEM((ROW_HALO, WIDTH), F32),
            pltpu.VMEM((ROW_HALO + tm, WIDTH), F32),
        ],
        compiler_params=_params(("arbitrary",)),
        name="merge",
    )(oa, oc, proj, proj, proj, proj, proj, proj, h, conv_sc, w_a, w_b, w_c, w_o)


def _ffn_kernel(h_ref, g_ref, wup_ref, convw_ref, wdown_ref, gf_ref, out_ref, halo_ref, xs_ref, act_ref,
                *, tiles_per_seq, final_norm):
    tile = h_ref.shape[0]
    tf = FFN_TILE

    @pl.when(pl.program_id(0) % tiles_per_seq == 0)
    def _():
        halo_ref[...] = jnp.zeros_like(halo_ref)

    x = h_ref[...]
    xn = _rms(x, g_ref[...]).astype(BF16)
    def stage(j):
        for part in range(2):
            c0 = part * D_FF + j * tf
            _conv_stage(_dot(xn, wup_ref[:, c0:c0 + tf]), halo_ref.at[2 * j + part], xs_ref.at[2 * j + part])

    def finish(j):
        gate, up = (_conv_apply(xs_ref.at[2 * j + part], convw_ref[:, part * D_FF + j * tf:part * D_FF + (j + 1) * tf])
                    for part in range(2))
        act_ref[:, j * tf:(j + 1) * tf] = (gate * _sigmoid(gate) * up).astype(BF16)

    n_tiles = D_FF // tf
    stage(0)
    for j in range(n_tiles):
        if j + 1 < n_tiles:
            stage(j + 1)
        finish(j)
    y = x + _dot(act_ref[...], wdown_ref[...])
    if final_norm:
        y = _rms(y, gf_ref[...])
    out_ref[...] = y


def _ffn(h, g, w_up, conv_ffn, w_down, g_final, layer, tm, seq, final_norm):
    n = h.shape[0]
    return pl.pallas_call(
        functools.partial(_ffn_kernel, tiles_per_seq=seq // tm, final_norm=final_norm),
        grid=(n // tm,),
        in_specs=[
            pl.BlockSpec((tm, D_MODEL), lambda i: (i, 0)),
            _resident((1, D_MODEL)),
            _layer_resident((D_MODEL, 2 * D_FF), layer),
            _resident((FFN_CONV, 2 * D_FF)),
            _layer_resident((D_FF, D_MODEL), layer),
            _resident((1, D_MODEL)),
        ],
        out_specs=pl.BlockSpec((tm, D_MODEL), lambda i: (i, 0)),
        out_shape=jax.ShapeDtypeStruct((n, D_MODEL), F32),
        scratch_shapes=[
            pltpu.VMEM((2 * (D_FF // FFN_TILE), ROW_HALO, FFN_TILE), F32),
            pltpu.VMEM((2 * (D_FF // FFN_TILE), ROW_HALO + tm, FFN_TILE), F32),
            pltpu.VMEM((tm, D_FF), BF16),
        ],
        compiler_params=_params(("arbitrary",)),
        name="ffn",
    )(h, g, w_up, conv_ffn, w_down, g_final)


def _reorder_kernel(main_ref, next_ref, s1_ref, s2_ref, big_ref, small_ref):
    j = pl.program_id(1)
    cat = jnp.concatenate([main_ref[0], next_ref[0]], axis=1)

    def emit(shift):
        big_ref[0] = cat[:, shift:shift + D_MODEL].astype(BF16)

    pl.when(j < COL_BCH // D_MODEL)(lambda: emit(0))
    pl.when((j >= COL_BCH // D_MODEL) & (j < COL_GATES // D_MODEL))(lambda: emit(2 * N_HEADS))
    pl.when(j >= COL_GATES // D_MODEL)(lambda: emit(3 * N_HEADS))
    lane = lax.broadcasted_iota(jnp.int32, s1_ref.shape[1:], 1)
    small_ref[0] = jnp.where(lane < LANE_F, s1_ref[0], jnp.where(lane < LANE_F + N_HEADS, s2_ref[0], 0.0)).astype(BF16)


def _reorder_w_in(w_in):
    depth, d, width = w_in.shape
    scalars_1 = COL_BCH
    scalars_2 = COL_GATES + 2 * N_HEADS
    assert width == PROJ_COLS + 3 * N_HEADS and COL_BCH % D_MODEL == 0 and COL_GATES % D_MODEL == 0
    assert scalars_1 % SMALL_COLS == LANE_BETA and scalars_2 % SMALL_COLS == LANE_F and LANE_F == 2 * N_HEADS
    per_block = D_MODEL // SMALL_COLS
    return pl.pallas_call(
        _reorder_kernel,
        grid=(depth, PROJ_COLS // D_MODEL),
        in_specs=[
            pl.BlockSpec((1, d, D_MODEL), lambda l, j: (l, 0, j)),
            pl.BlockSpec((1, d, SMALL_COLS), lambda l, j: (l, 0, (j + 1) * per_block)),
            pl.BlockSpec((1, d, SMALL_COLS), lambda l, j: (l, 0, scalars_1 // SMALL_COLS)),
            pl.BlockSpec((1, d, SMALL_COLS), lambda l, j: (l, 0, scalars_2 // SMALL_COLS)),
        ],
        out_specs=[
            pl.BlockSpec((1, d, D_MODEL), lambda l, j: (l, 0, j)),
            pl.BlockSpec((1, d, SMALL_COLS), lambda l, j: (l, 0, 0)),
        ],
        out_shape=[
            jax.ShapeDtypeStruct((depth, d, PROJ_COLS), BF16),
            jax.ShapeDtypeStruct((depth, d, SMALL_COLS), BF16),
        ],
        compiler_params=_params(("arbitrary", "arbitrary")),
        name="reorder_w_in",
    )(w_in, w_in, w_in, w_in)


def _lane_row(pairs):
    row = jnp.zeros((1, SMALL_COLS), F32)
    for lane0, vals in pairs:
        row = row.at[0, lane0:lane0 + N_HEADS].set(vals.astype(F32))
    return row


def kernel(x, norm1_g, w_in, conv_qkv, a_log, dt_bias, gdn_norm, w_br_a, conv_sc, w_br_b, fox_bias, w_br_c, w_o, norm2_g, w_up, conv_ffn, w_down, norm_f):
    batch, seq, d = x.shape
    assert d == D_MODEL and seq % TOKEN_TILE == 0 and seq % GATES_TILE == 0 and seq % FOX_K_TILE == 0 and seq % MERGE_TILE == 0 and seq % GDN_TILE == 0
    tm = TOKEN_TILE
    h = x.reshape(batch * seq, d)
    w_big, w_small = _reorder_w_in(w_in)
    w_a, w_b, w_c, w_out = (w.astype(BF16) for w in (w_br_a, w_br_b, w_br_c, w_o))
    w_up16, w_down16 = w_up.astype(BF16), w_down.astype(BF16)
    for l in range(DEPTH):
        proj, small = _inproj(h, norm1_g[l][None, :], w_big, w_small, conv_qkv[l], l, tm, seq)
        bias_row = _lane_row([(LANE_G, dt_bias[l]), (LANE_F, fox_bias[l])])
        alog_row = _lane_row([(LANE_G, a_log[l])])
        gate = _gates(small, bias_row, alog_row, batch, seq)
        oa = _gdn(proj, gate, gdn_norm[l][None, :], batch, seq)
        oc = _fox(proj, gate, batch, seq)
        h = _merge(oa, oc, proj, h, conv_sc[l], w_a, w_b, w_c, w_out, l, MERGE_TILE, seq)
        h = _ffn(h, norm2_g[l][None, :], w_up16, conv_ffn[l], w_down16, norm_f[None, :], l, tm, seq,
                 final_norm=(l == DEPTH - 1))
    return h.reshape(batch, seq, d)
```

```python
import functools

import jax
import jax.numpy as jnp
from jax import lax
from jax.experimental import pallas as pl
from jax.experimental.pallas import tpu as pltpu

F32 = jnp.float32
BF16 = jnp.bfloat16

D_MODEL = 1024
DEPTH = 4
N_HEADS = 4
HEAD_DIM = 128
WIDTH = N_HEADS * HEAD_DIM
GDN_CONV = 4
GDN_CHUNK = 64
SC_CONV = 3
D_FF = 2816
FFN_CONV = 3
EPS = 1e-6

COL_QKV_A = 0
COL_Z_A = 3 * WIDTH
COL_BCH = 4 * WIDTH
COL_QKV_C = 7 * WIDTH
COL_GATES = 10 * WIDTH
PROJ_COLS = 10 * WIDTH + 3 * D_MODEL
LANE_BETA = 0
LANE_G = 4
LANE_F = 8
SMALL_COLS = 128

ROW_HALO = 8
TOKEN_TILE = 512
MERGE_TILE = 1024
GDN_TILE = 512
QKV_TILE = 256
FOX_Q_TILE = 256
FOX_K_TILE = 256
FFN_TILE = 256
VMEM_LIMIT = 56 * 1024 * 1024


def _resident(shape):
    nd = len(shape)
    return pl.BlockSpec(shape, lambda *_: (0,) * nd, pipeline_mode=pl.Buffered(1))


def _layer_resident(shape, layer):
    nd = len(shape)
    return pl.BlockSpec((None,) + shape, lambda *_: (layer,) + (0,) * nd, pipeline_mode=pl.Buffered(1))


def _params(sem):
    return pltpu.CompilerParams(dimension_semantics=sem, vmem_limit_bytes=VMEM_LIMIT)


def _rms(x, g):
    return x * lax.rsqrt(jnp.mean(x * x, axis=-1, keepdims=True) + EPS) * g


LOG2_E = 1.4426950408889634
FOX_Q_SCALE = HEAD_DIM ** -0.5 * LOG2_E


def _sigmoid(x):
    return 1.0 / (1.0 + jnp.exp2(x * -LOG2_E))


def _dot(a, b):
    return jnp.dot(a, b, preferred_element_type=F32)


def _dot_nt(a, b):
    return lax.dot_general(a, b, (((1,), (1,)), ((), ())), preferred_element_type=F32)


def _dot_tn(a, b):
    return lax.dot_general(a, b, (((0,), (0,)), ((), ())), preferred_element_type=F32)


def _bdot(a, b):
    return lax.dot_general(a, b, (((2,), (1,)), ((0,), (0,))), preferred_element_type=F32)


def _bdot_nt(a, b):
    return lax.dot_general(a, b, (((2,), (2,)), ((0,), (0,))), preferred_element_type=F32)


def _split3(x):
    h1 = x.astype(BF16)
    r1 = x - h1.astype(F32)
    h2 = r1.astype(BF16)
    h3 = (r1 - h2.astype(F32)).astype(BF16)
    return h1, h2, h3


def _bdot_f32(a, b):
    a_hi = a.astype(BF16).astype(F32)
    b_hi = b.astype(BF16).astype(F32)
    lhs = jnp.concatenate([a_hi, a - a_hi, a_hi], axis=-1).astype(BF16)
    rhs = jnp.concatenate([b_hi, b_hi, b - b_hi], axis=1).astype(BF16)
    return _bdot(lhs, rhs)


def _diff_operands(c):
    shape = c.shape[:-1] + (HEAD_DIM,)
    lane = lax.broadcasted_iota(jnp.int32, shape, len(shape) - 1)
    c1, c2, c3 = (term.astype(F32) for term in _split3(jnp.broadcast_to(c, shape)))
    terms = jnp.where((lane == 0) | (lane == 3), c1, jnp.where((lane == 1) | (lane == 4), c2, c3))
    a = jnp.where(lane < 3, terms, jnp.where(lane < 6, 1.0, 0.0))
    b = jnp.where(lane < 3, 1.0, jnp.where(lane < 6, -terms, 0.0))
    return a.astype(BF16), b.astype(BF16)


def _conv_stage(pre, halo_ref, xs_ref):
    tile = pre.shape[0]
    xs_ref[0:ROW_HALO, :] = halo_ref[...]
    xs_ref[ROW_HALO:ROW_HALO + tile, :] = pre
    halo_ref[...] = pre[tile - ROW_HALO:tile, :]


def _conv_apply(xs_ref, taps):
    ntap = taps.shape[0]
    staged = xs_ref[...]
    prev = pltpu.roll(staged, 1, axis=0)
    out = None
    for pair in range((ntap + 1) // 2):
        k = ntap - 1 - 2 * pair
        term = staged * taps[k:k + 1, :]
        if k > 0:
            term = term + prev * taps[k - 1:k, :]
        if pair > 0:
            term = pltpu.roll(term, 2 * pair, axis=0)
        out = term if out is None else out + term
    return out[ROW_HALO:, :]


def _inproj_kernel(x_ref, g_ref, w_ref, ws_ref, convw_ref, bias_ref, alog_ref, proj_ref, gate_ref,
                   halo_ref, xs_ref, carry_ref, *, tiles_per_seq):
    @pl.when(pl.program_id(0) % tiles_per_seq == 0)
    def _():
        halo_ref[...] = jnp.zeros_like(halo_ref)
        carry_ref[...] = jnp.zeros_like(carry_ref)

    xn = _rms(x_ref[...], g_ref[...]).astype(BF16)

    def stage(t):
        c0 = COL_QKV_A + t * QKV_TILE
        _conv_stage(_dot(xn, w_ref[:, c0:c0 + QKV_TILE]), halo_ref.at[t], xs_ref.at[t])

    def epilogue(t):
        c0 = COL_QKV_A + t * QKV_TILE
        part = (t * QKV_TILE) // WIDTH
        y = _conv_apply(xs_ref.at[t], convw_ref[:, c0:c0 + QKV_TILE])
        y = y * _sigmoid(y)
        for lo in range(0, QKV_TILE, HEAD_DIM):
            yh = y[:, lo:lo + HEAD_DIM]
            if part < 2:
                yh = yh * lax.rsqrt(jnp.sum(yh * yh, axis=-1, keepdims=True) + EPS)
            if part == 0:
                yh = yh * (HEAD_DIM ** -0.5)
            proj_ref[:, c0 + lo:c0 + lo + HEAD_DIM] = yh.astype(BF16)

    plain = [(COL_Z_A, COL_BCH)] + [(c0, c0 + D_MODEL) for c0 in range(COL_BCH, PROJ_COLS, D_MODEL)]
    n_qkv = 3 * WIDTH // QKV_TILE
    stage(0)
    for t in range(max(n_qkv, len(plain))):
        if t + 1 < n_qkv:
            stage(t + 1)
        if t < len(plain):
            c0, c1 = plain[t]
            y = _dot(xn, w_ref[:, c0:c1])
            if c0 <= COL_QKV_C < c1:
                lo = COL_QKV_C - c0
                assert lo + WIDTH <= c1 - c0
                pieces = [y[:, 0:lo], y[:, lo:lo + WIDTH] * FOX_Q_SCALE, y[:, lo + WIDTH:]]
                y = jnp.concatenate([piece for piece in pieces if piece.shape[1] > 0], axis=1)
            proj_ref[:, c0:c1] = y.astype(BF16)
        if t < n_qkv:
            epilogue(t)
    gate_ref[...] = _gate_scalars(_dot(xn, ws_ref[...]), bias_ref, alog_ref, carry_ref)


def _inproj(h, g, w_big, w_small, conv_qkv, bias_row, alog_row, layer, tm, seq):
    n = h.shape[0]
    return pl.pallas_call(
        functools.partial(_inproj_kernel, tiles_per_seq=seq // tm),
        grid=(n // tm,),
        in_specs=[
            pl.BlockSpec((tm, D_MODEL), lambda i: (i, 0)),
            _resident((1, D_MODEL)),
            _layer_resident((D_MODEL, PROJ_COLS), layer),
            _layer_resident((D_MODEL, SMALL_COLS), layer),
            _resident((GDN_CONV, 3 * WIDTH)),
            _resident((1, SMALL_COLS)),
            _resident((1, SMALL_COLS)),
        ],
        out_specs=[
            pl.BlockSpec((tm, PROJ_COLS), lambda i: (i, 0)),
            pl.BlockSpec((tm, SMALL_COLS), lambda i: (i, 0)),
        ],
        out_shape=[
            jax.ShapeDtypeStruct((n, PROJ_COLS), BF16),
            jax.ShapeDtypeStruct((n, SMALL_COLS), F32),
        ],
        scratch_shapes=[
            pltpu.VMEM((3 * WIDTH // QKV_TILE, ROW_HALO, QKV_TILE), F32),
            pltpu.VMEM((3 * WIDTH // QKV_TILE, ROW_HALO + tm, QKV_TILE), F32),
            pltpu.VMEM((ROW_HALO, SMALL_COLS), F32),
        ],
        compiler_params=_params(("arbitrary",)),
        name="inproj",
    )(h, g, w_big, w_small, conv_qkv, bias_row, alog_row)


def _gate_scalars(small, bias_ref, alog_ref, carry_ref):
    tile = small.shape[0]
    x = small + bias_ref[...]
    lane = lax.broadcasted_iota(jnp.int32, x.shape, 1)
    soft = jnp.log1p(jnp.exp(-jnp.abs(x)))
    beta = _sigmoid(x)
    g = -jnp.exp(alog_ref[...]) * (jnp.maximum(x, 0.0) + soft)
    logf = jnp.minimum(x, 0.0) - soft
    vals = jnp.where(lane < LANE_G, beta, jnp.where(lane < LANE_F, g, logf))

    row = lax.broadcasted_iota(jnp.int32, (tile, tile), 0)
    col = lax.broadcasted_iota(jnp.int32, (tile, tile), 1)
    tril = row >= col
    m_full = tril.astype(BF16)
    m_seg = (tril & (row // GDN_CHUNK == col // GDN_CHUNK)).astype(BF16)
    cum = _dot(jnp.concatenate([m_full, m_seg], axis=0), jnp.concatenate(_split3(vals), axis=1))
    cum = cum[:, 0:SMALL_COLS] + (cum[:, SMALL_COLS:2 * SMALL_COLS] + cum[:, 2 * SMALL_COLS:3 * SMALL_COLS])
    cum_full = cum[0:tile] + carry_ref[0:1, :]
    cum_seg = cum[tile:2 * tile]
    carry_ref[0:1, :] = cum_full[tile - 1:tile, :]
    return jnp.where(lane < LANE_G, beta, jnp.where(lane < LANE_F, cum_seg, cum_full))


def _unit_lower_inverse(l_strict):
    c = l_strict.shape[-1]
    row = lax.broadcasted_iota(jnp.int32, (c, c), 0)
    col = lax.broadcasted_iota(jnp.int32, (c, c), 1)
    power = -l_strict
    prod = jnp.where(row == col, 1.0, 0.0)[None] + power
    span = 1
    while 2 * span < c:
        power = _bdot_f32(power, power) if span == 1 else stacked[:, :c]
        span *= 2
        if 2 * span < c:
            stacked = _bdot_f32(jnp.concatenate([power, prod], axis=1), power)
            prod = prod + stacked[:, c:]
        else:
            prod = prod + _bdot_f32(prod, power)
    return prod


def _gdn_kernel(qkv_ref, z_ref, gate_ref, norm_ref, o_ref, state_ref):
    tile = qkv_ref.shape[0]

    @pl.when(pl.program_id(1) == 0)
    def _():
        state_ref[...] = jnp.zeros_like(state_ref)

    c = GDN_CHUNK
    nchunk = tile // c
    problems = [(j, h) for j in range(nchunk) for h in range(N_HEADS)]

    def slabs(col0):
        return jnp.stack([qkv_ref[j * c:(j + 1) * c, col0 + h * HEAD_DIM:col0 + (h + 1) * HEAD_DIM]
                          for j, h in problems])

    def lanes(lane0, rows=None):
        return jnp.stack([gate_ref[(j * c if rows is None else j * c + rows):(j + 1) * c,
                                   lane0 + h:lane0 + h + 1] for j, h in problems])

    q = slabs(0).astype(F32)
    k16 = slabs(WIDTH)
    k = k16.astype(F32)
    v = slabs(2 * WIDTH).astype(F32)
    beta = lanes(LANE_BETA)
    gc = lanes(LANE_G)
    g_last = lanes(LANE_G, rows=c - 1)

    row = lax.broadcasted_iota(jnp.int32, (c, c), 0)
    col = lax.broadcasted_iota(jnp.int32, (c, c), 1)
    da, db = _diff_operands(gc)
    decay = jnp.exp(jnp.where((row >= col)[None], _bdot_nt(da, db), -jnp.inf))
    kb = k * beta
    kk_qk = _bdot_nt(jnp.concatenate([kb.astype(BF16), slabs(0)], axis=1), k16)
    l_strict = jnp.where((row > col)[None], kk_qk[:, 0:c] * decay, 0.0)
    attn = (kk_qk[:, c:2 * c] * decay).astype(BF16)
    tm = _unit_lower_inverse(l_strict).astype(BF16)
    uw = _bdot(tm, jnp.concatenate([v * beta, kb * jnp.exp(gc)], axis=-1).astype(BF16))
    u = uw[..., 0:HEAD_DIM]
    w = uw[..., HEAD_DIM:2 * HEAD_DIM].astype(BF16)
    qg = (q * jnp.exp(gc)).astype(BF16)
    kd = (k * jnp.exp(g_last - gc)).astype(BF16)
    s_decay = jnp.exp(g_last)

    u_hi = u.astype(BF16)
    u_lo = (u - u_hi.astype(F32)).astype(BF16)
    wuu = jnp.concatenate([w, u_hi, u_lo], axis=-1)
    kd_t = jnp.stack([_dot_tn(kd[p], wuu[p]) for p in range(len(problems))])
    kd_w = kd_t[..., 0:HEAD_DIM]
    kd_u = kd_t[..., HEAD_DIM:2 * HEAD_DIM] + kd_t[..., 2 * HEAD_DIM:3 * HEAD_DIM]
    kd_w_hi = kd_w.astype(BF16)
    kd_w_split = jnp.concatenate([kd_w_hi, (kd_w - kd_w_hi.astype(F32)).astype(BF16)], axis=-1)

    s = state_ref[...]
    starts = []
    for j in range(nchunk):
        b0, b1 = j * N_HEADS, (j + 1) * N_HEADS
        s16 = s.astype(BF16)
        starts.append(s16)
        s = s * s_decay[b0:b1] + kd_u[b0:b1] - _bdot(kd_w_split[b0:b1], jnp.concatenate([s16, s16], axis=1))
    state_ref[...] = s

    s_start = jnp.concatenate(starts, axis=0)
    wq_s = _bdot(jnp.concatenate([w, qg], axis=1), s_start)
    v_new = (u - wq_s[:, 0:c]).astype(BF16)
    o = _rms(wq_s[:, c:2 * c] + _bdot(attn, v_new), norm_ref[...][None])
    for p, (j, h) in enumerate(problems):
        lo = h * HEAD_DIM
        z = z_ref[j * c:(j + 1) * c, lo:lo + HEAD_DIM].astype(F32)
        o_ref[j * c:(j + 1) * c, lo:lo + HEAD_DIM] = (o[p] * (z * _sigmoid(z))).astype(BF16)


def _gdn(proj, gate, gdn_norm, batch, seq):
    tile = GDN_TILE
    nt = seq // tile
    n = proj.shape[0]
    return pl.pallas_call(
        _gdn_kernel,
        grid=(batch, nt),
        in_specs=[
            pl.BlockSpec((tile, 3 * WIDTH), lambda b, t: (b * nt + t, COL_QKV_A // (3 * WIDTH))),
            pl.BlockSpec((tile, WIDTH), lambda b, t: (b * nt + t, COL_Z_A // WIDTH)),
            pl.BlockSpec((tile, SMALL_COLS), lambda b, t: (b * nt + t, 0)),
            _resident((1, HEAD_DIM)),
        ],
        out_specs=pl.BlockSpec((tile, WIDTH), lambda b, t: (b * nt + t, 0)),
        out_shape=jax.ShapeDtypeStruct((n, WIDTH), BF16),
        scratch_shapes=[pltpu.VMEM((N_HEADS, HEAD_DIM, HEAD_DIM), F32)],
        compiler_params=_params(("arbitrary", "arbitrary")),
        name="gdn",
    )(proj, proj, gate, gdn_norm)


def _fox_kernel(q_ref, k_ref, v_ref, gate_ref, o_ref, qa_ref, ka_ref, va_ref):
    seq = q_ref.shape[0]
    tq, tk = FOX_Q_TILE, FOX_K_TILE
    lane = lax.broadcasted_iota(jnp.int32, (seq, SMALL_COLS), 1)
    c = jnp.sum(jnp.where(lane == LANE_F + pl.program_id(1), gate_ref[...], 0.0), axis=-1, keepdims=True)
    ca, cb = _diff_operands(c * LOG2_E)
    qa_ref[:, 0:HEAD_DIM] = q_ref[...]
    qa_ref[:, HEAD_DIM:2 * HEAD_DIM] = ca
    ka_ref[:, 0:HEAD_DIM] = k_ref[...]
    ka_ref[:, HEAD_DIM:2 * HEAD_DIM] = cb
    va_ref[:, 0:HEAD_DIM] = v_ref[...]
    va_ref[:, HEAD_DIM:2 * HEAD_DIM] = jnp.ones((seq, HEAD_DIM), BF16)
    row = lax.broadcasted_iota(jnp.int32, (tq, tk), 0)
    col = lax.broadcasted_iota(jnp.int32, (tq, tk), 1)
    n_q = seq // tq
    m = [jnp.full((tq, 1), -jnp.inf, F32) for _ in range(n_q)]
    acc = [jnp.zeros((tq, 2 * HEAD_DIM), F32) for _ in range(n_q)]
    for j in range(seq // tk):
        kblk = ka_ref[j * tk:(j + 1) * tk, :]
        vblk = va_ref[j * tk:(j + 1) * tk, :]
        for i in range((j * tk) // tq, n_q):
            s = _dot_nt(qa_ref[i * tq:(i + 1) * tq, :], kblk)
            if (i * tq) // tk == j:
                s = jnp.where(row + (i * tq - j * tk) >= col, s, -jnp.inf)
            m_new = jnp.maximum(m[i], jnp.max(s, axis=-1, keepdims=True))
            p = jnp.exp2(s - m_new)
            alpha = jnp.exp2(m[i] - m_new)
            acc[i] = alpha * acc[i] + _dot(p.astype(BF16), vblk)
            m[i] = m_new
            if (i * tq) // tk == j:
                o_ref[i * tq:(i + 1) * tq, :] = (acc[i][:, 0:HEAD_DIM] / acc[i][:, HEAD_DIM:2 * HEAD_DIM]).astype(BF16)


def _fox(proj, gate, batch, seq):
    n = proj.shape[0]
    blk = COL_QKV_C // HEAD_DIM
    return pl.pallas_call(
        _fox_kernel,
        grid=(batch, N_HEADS),
        in_specs=[
            pl.BlockSpec((seq, HEAD_DIM), lambda b, h: (b, blk + h)),
            pl.BlockSpec((seq, HEAD_DIM), lambda b, h: (b, blk + N_HEADS + h)),
            pl.BlockSpec((seq, HEAD_DIM), lambda b, h: (b, blk + 2 * N_HEADS + h)),
            pl.BlockSpec((seq, SMALL_COLS), lambda b, h: (b, 0)),
        ],
        out_specs=pl.BlockSpec((seq, HEAD_DIM), lambda b, h: (b, h)),
        out_shape=jax.ShapeDtypeStruct((n, WIDTH), BF16),
        scratch_shapes=[pltpu.VMEM((seq, 2 * HEAD_DIM), BF16)] * 3,
        compiler_params=_params(("arbitrary", "arbitrary")),
        name="fox",
    )(proj, proj, proj, gate)


def _merge_kernel(oa_ref, oc_ref, bg_ref, cg_ref, hh_ref, ga_ref, gb_ref, gc_ref, h_ref,
                  convw_ref, wa_ref, wb_ref, wc_ref, wo_ref, out_ref, halo_ref, xs_ref,
                  *, tiles_per_seq):
    tile = h_ref.shape[0]

    @pl.when(pl.program_id(0) % tiles_per_seq == 0)
    def _():
        halo_ref[...] = jnp.zeros_like(halo_ref)

    _conv_stage(cg_ref[...].astype(F32) * hh_ref[...].astype(F32), halo_ref, xs_ref)
    conv = _conv_apply(xs_ref, convw_ref[...])
    sc = (bg_ref[...].astype(F32) * conv).astype(BF16)

    mix = _sigmoid(ga_ref[...].astype(F32)) * _dot(oa_ref[...], wa_ref[...])
    mix = mix + _sigmoid(gb_ref[...].astype(F32)) * _dot(sc, wb_ref[...])
    mix = mix + _sigmoid(gc_ref[...].astype(F32)) * _dot(oc_ref[...], wc_ref[...])
    out_ref[...] = h_ref[...] + _dot(mix.astype(BF16), wo_ref[...])


def _merge(oa, oc, proj, h, conv_sc, w_a, w_b, w_c, w_o, layer, tm, seq):
    n = h.shape[0]
    bch = COL_BCH // WIDTH
    gates = COL_GATES // D_MODEL
    tok = lambda width, blk: pl.BlockSpec((tm, width), lambda i: (i, blk))
    return pl.pallas_call(
        functools.partial(_merge_kernel, tiles_per_seq=seq // tm),
        grid=(n // tm,),
        in_specs=[
            tok(WIDTH, 0), tok(WIDTH, 0),
            tok(WIDTH, bch), tok(WIDTH, bch + 1), tok(WIDTH, bch + 2),
            tok(D_MODEL, gates), tok(D_MODEL, gates + 1), tok(D_MODEL, gates + 2),
            tok(D_MODEL, 0),
            _resident((SC_CONV, WIDTH)),
            _layer_resident((WIDTH, D_MODEL), layer), _layer_resident((WIDTH, D_MODEL), layer),
            _layer_resident((WIDTH, D_MODEL), layer), _layer_resident((D_MODEL, D_MODEL), layer),
        ],
        out_specs=tok(D_MODEL, 0),
        out_shape=jax.ShapeDtypeStruct((n, D_MODEL), F32),
        scratch_shapes=[
            pltpu.VMEM((ROW_HALO, WIDTH), F32),
            pltpu.VMEM((ROW_HALO + tm, WIDTH), F32),
        ],
        compiler_params=_params(("arbitrary",)),
        name="merge",
    )(oa, oc, proj, proj, proj, proj, proj, proj, h, conv_sc, w_a, w_b, w_c, w_o)


def _ffn_kernel(h_ref, g_ref, wup_ref, convw_ref, wdown_ref, gf_ref, out_ref, halo_ref, xs_ref, act_ref,
                *, tiles_per_seq, final_norm):
    tile = h_ref.shape[0]
    tf = FFN_TILE

    @pl.when(pl.program_id(0) % tiles_per_seq == 0)
    def _():
        halo_ref[...] = jnp.zeros_like(halo_ref)

    x = h_ref[...]
    xn = _rms(x, g_ref[...]).astype(BF16)
    def stage(j):
        for part in range(2):
            c0 = part * D_FF + j * tf
            _conv_stage(_dot(xn, wup_ref[:, c0:c0 + tf]), halo_ref.at[2 * j + part], xs_ref.at[2 * j + part])

    def finish(j):
        gate, up = (_conv_apply(xs_ref.at[2 * j + part], convw_ref[:, part * D_FF + j * tf:part * D_FF + (j + 1) * tf])
                    for part in range(2))
        act_ref[:, j * tf:(j + 1) * tf] = (gate * _sigmoid(gate) * up).astype(BF16)

    n_tiles = D_FF // tf
    stage(0)
    for j in range(n_tiles):
        if j + 1 < n_tiles:
            stage(j + 1)
        finish(j)
    y = x + _dot(act_ref[...], wdown_ref[...])
    if final_norm:
        y = _rms(y, gf_ref[...])
    out_ref[...] = y


def _ffn(h, g, w_up, conv_ffn, w_down, g_final, layer, tm, seq, final_norm):
    n = h.shape[0]
    return pl.pallas_call(
        functools.partial(_ffn_kernel, tiles_per_seq=seq // tm, final_norm=final_norm),
        grid=(n // tm,),
        in_specs=[
            pl.BlockSpec((tm, D_MODEL), lambda i: (i, 0)),
            _resident((1, D_MODEL)),
            _layer_resident((D_MODEL, 2 * D_FF), layer),
            _resident((FFN_CONV, 2 * D_FF)),
            _layer_resident((D_FF, D_MODEL), layer),
            _resident((1, D_MODEL)),
        ],
        out_specs=pl.BlockSpec((tm, D_MODEL), lambda i: (i, 0)),
        out_shape=jax.ShapeDtypeStruct((n, D_MODEL), F32),
        scratch_shapes=[
            pltpu.VMEM((2 * (D_FF // FFN_TILE), ROW_HALO, FFN_TILE), F32),
            pltpu.VMEM((2 * (D_FF // FFN_TILE), ROW_HALO + tm, FFN_TILE), F32),
            pltpu.VMEM((tm, D_FF), BF16),
        ],
        compiler_params=_params(("arbitrary",)),
        name="ffn",
    )(h, g, w_up, conv_ffn, w_down, g_final)


def _reorder_kernel(main_ref, next_ref, s1_ref, s2_ref, big_ref, small_ref):
    j = pl.program_id(1)
    cat = jnp.concatenate([main_ref[0], next_ref[0]], axis=1)

    def emit(shift):
        big_ref[0] = cat[:, shift:shift + D_MODEL].astype(BF16)

    pl.when(j < COL_BCH // D_MODEL)(lambda: emit(0))
    pl.when((j >= COL_BCH // D_MODEL) & (j < COL_GATES // D_MODEL))(lambda: emit(2 * N_HEADS))
    pl.when(j >= COL_GATES // D_MODEL)(lambda: emit(3 * N_HEADS))
    lane = lax.broadcasted_iota(jnp.int32, s1_ref.shape[1:], 1)
    small_ref[0] = jnp.where(lane < LANE_F, s1_ref[0], jnp.where(lane < LANE_F + N_HEADS, s2_ref[0], 0.0)).astype(BF16)


def _reorder_w_in(w_in):
    depth, d, width = w_in.shape
    scalars_1 = COL_BCH
    scalars_2 = COL_GATES + 2 * N_HEADS
    assert width == PROJ_COLS + 3 * N_HEADS and COL_BCH % D_MODEL == 0 and COL_GATES % D_MODEL == 0
    assert scalars_1 % SMALL_COLS == LANE_BETA and scalars_2 % SMALL_COLS == LANE_F and LANE_F == 2 * N_HEADS
    per_block = D_MODEL // SMALL_COLS
    return pl.pallas_call(
        _reorder_kernel,
        grid=(depth, PROJ_COLS // D_MODEL),
        in_specs=[
            pl.BlockSpec((1, d, D_MODEL), lambda l, j: (l, 0, j)),
            pl.BlockSpec((1, d, SMALL_COLS), lambda l, j: (l, 0, (j + 1) * per_block)),
            pl.BlockSpec((1, d, SMALL_COLS), lambda l, j: (l, 0, scalars_1 // SMALL_COLS)),
            pl.BlockSpec((1, d, SMALL_COLS), lambda l, j: (l, 0, scalars_2 // SMALL_COLS)),
        ],
        out_specs=[
            pl.BlockSpec((1, d, D_MODEL), lambda l, j: (l, 0, j)),
            pl.BlockSpec((1, d, SMALL_COLS), lambda l, j: (l, 0, 0)),
        ],
        out_shape=[
            jax.ShapeDtypeStruct((depth, d, PROJ_COLS), BF16),
            jax.ShapeDtypeStruct((depth, d, SMALL_COLS), BF16),
        ],
        compiler_params=_params(("arbitrary", "arbitrary")),
        name="reorder_w_in",
    )(w_in, w_in, w_in, w_in)


def _lane_row(pairs):
    row = jnp.zeros((1, SMALL_COLS), F32)
    for lane0, vals in pairs:
        row = row.at[0, lane0:lane0 + N_HEADS].set(vals.astype(F32))
    return row


def kernel(x, norm1_g, w_in, conv_qkv, a_log, dt_bias, gdn_norm, w_br_a, conv_sc, w_br_b, fox_bias, w_br_c, w_o, norm2_g, w_up, conv_ffn, w_down, norm_f):
    batch, seq, d = x.shape
    assert d == D_MODEL and seq % TOKEN_TILE == 0 and seq % FOX_K_TILE == 0 and seq % MERGE_TILE == 0 and seq % GDN_TILE == 0
    tm = TOKEN_TILE
    h = x.reshape(batch * seq, d)
    w_big, w_small = _reorder_w_in(w_in)
    w_a, w_b, w_c, w_out = (w.astype(BF16) for w in (w_br_a, w_br_b, w_br_c, w_o))
    w_up16, w_down16 = w_up.astype(BF16), w_down.astype(BF16)
    for l in range(DEPTH):
        bias_row = _lane_row([(LANE_G, dt_bias[l]), (LANE_F, fox_bias[l])])
        alog_row = _lane_row([(LANE_G, a_log[l])])
        proj, gate = _inproj(h, norm1_g[l][None, :], w_big, w_small, conv_qkv[l], bias_row, alog_row, l, tm, seq)
        oa = _gdn(proj, gate, gdn_norm[l][None, :], batch, seq)
        oc = _fox(proj, gate, batch, seq)
        h = _merge(oa, oc, proj, h, conv_sc[l], w_a, w_b, w_c, w_out, l, MERGE_TILE, seq)
        h = _ffn(h, norm2_g[l][None, :], w_up16, conv_ffn[l], w_down16, norm_f[None, :], l, tm, seq,
                 final_norm=(l == DEPTH - 1))
    return h.reshape(batch, seq, d)
```
